```python
import math
import jax, jax.numpy as jnp
from jax import lax
import numpy as np

D_MODEL = 2048
BATCH = 16
SEQ = 2048
DEPTH = 4

HEAD_DIM = 128
MIX_WIDTH = D_MODEL
DSA_HEADS = 4
IDX_HEADS = 16
IDX_DIM = 64
IDX_ROPE = 32
DSA_TOPK = 256
DSA_BLOCK = 32
DIFF_HEADS = 4
DIFF_QK = 64
DIL_HEADS = 4
DIL_PATTERNS = ((128, 1), (512, 4), (2048, 16))
BAND = 128
MLA_HEADS = 4
MLA_Q_RANK = 384
MLA_KV_RANK = 256
MLA_NOPE = 128
MLA_ROPE = 64
MLA_V = 128
T5_BUCKETS = 32
T5_MAX_DIST = 2048
N_BIAS_HEADS = DSA_HEADS + DIFF_HEADS + DIL_HEADS
ROPE_THETA = 10000.0
Q_BLOCK = 128
N_EXPERTS = 32
TOP_K = 4
D_EXPERT = 768
SWIGLU_ALPHA = 1.702
SWIGLU_LIMIT = 7.0
MOE_BLOCK = 128
DEEPNORM_ALPHA = (2 * DEPTH) ** 0.25
DEEPNORM_BETA = (8 * DEPTH) ** -0.25
EPS = 1e-5

IN_SPLITS = (
    DSA_HEADS * HEAD_DIM, DSA_HEADS * HEAD_DIM, DSA_HEADS * HEAD_DIM,
    IDX_HEADS * IDX_DIM, IDX_DIM, IDX_HEADS,
    DIFF_HEADS * 2 * DIFF_QK, DIFF_HEADS * 2 * DIFF_QK, DIFF_HEADS * HEAD_DIM,
    DIL_HEADS * HEAD_DIM, DIL_HEADS * HEAD_DIM, DIL_HEADS * HEAD_DIM,
    MLA_Q_RANK, MLA_KV_RANK, MLA_ROPE,
)
D_IN = sum(IN_SPLITS)

kernel_name = 'hybrid_dsa_diff_dilated_mla_moe_deepnorm'


def layer_norm(x, g, b):
    xf = x.astype(jnp.float32)
    mu = xf.mean(-1, keepdims=True)
    var = jnp.square(xf - mu).mean(-1, keepdims=True)
    return ((xf - mu) * lax.rsqrt(var + EPS) * g.astype(jnp.float32) + b.astype(jnp.float32)).astype(x.dtype)


def rms_norm(x, g):
    xf = x.astype(jnp.float32)
    return (xf * lax.rsqrt(jnp.square(xf).mean(-1, keepdims=True) + EPS) * g.astype(jnp.float32)).astype(x.dtype)


def rope(x, pos):
    half = x.shape[-1] // 2
    inv = ROPE_THETA ** (-jnp.arange(half, dtype=jnp.float32) / half)
    ang = pos.astype(jnp.float32)[..., None] * inv
    ang = ang.reshape(ang.shape[:2] + (1,) * (x.ndim - 3) + (half,))
    cos, sin = jnp.cos(ang), jnp.sin(ang)
    xf = x.astype(jnp.float32)
    x1, x2 = xf[..., :half], xf[..., half:]
    return jnp.concatenate([x1 * cos - x2 * sin, x1 * sin + x2 * cos], -1).astype(x.dtype)


def t5_bucket(dist):
    n = jnp.maximum(dist, 0)
    exact = T5_BUCKETS // 2
    nf = jnp.maximum(n, 1).astype(jnp.float32)
    large = exact + (jnp.log(nf / exact) / math.log(T5_MAX_DIST / exact) * (T5_BUCKETS - exact)).astype(jnp.int32)
    return jnp.where(n < exact, n, jnp.minimum(large, T5_BUCKETS - 1))


def _query_blocks(a, bs):
    B, S = a.shape[:2]
    return jnp.moveaxis(a.reshape((B, S // bs, bs) + a.shape[2:]), 1, 0)


def _unblock(a):
    a = jnp.moveaxis(a, 0, 1)
    return a.reshape((a.shape[0], a.shape[1] * a.shape[2]) + a.shape[3:])


def dsa_attention(q, k, v, q_idx, k_idx, w_idx, rel):
    B, S, H, hd = q.shape
    n_sel = min(DSA_TOPK, S // 4)
    key_pos = jnp.arange(S)
    w_idx = w_idx.astype(jnp.float32) * (IDX_HEADS ** -0.5 * IDX_DIM ** -0.5)
    gather = jax.vmap(lambda a, i: a[i])

    def block(args):
        qb, qib, wb, start = args
        qpos = start + jnp.arange(DSA_BLOCK)
        score = jax.nn.relu(jnp.einsum('bqhd,bsd->bqhs', qib, k_idx).astype(jnp.float32))
        score = jnp.einsum('bqhs,bqh->bqs', score, wb)
        score = jnp.where(key_pos[None, None, :] <= qpos[None, :, None], score, -jnp.inf)
        _, idx = lax.top_k(score, n_sel)
        k_sel = gather(k, idx)
        v_sel = gather(v, idx)
        dist = qpos[None, :, None] - idx
        logits = jnp.einsum('bqhd,bqkhd->bqhk', qb, k_sel).astype(jnp.float32) * hd ** -0.5
        logits = logits + jnp.moveaxis(rel[t5_bucket(dist)].astype(jnp.float32), -1, 2)
        logits = jnp.where((dist >= 0)[:, :, None, :], logits, -jnp.inf)
        p = jax.nn.softmax(logits, axis=-1).astype(v.dtype)
        return jnp.einsum('bqhk,bqkhd->bqhd', p, v_sel)

    nb = S // DSA_BLOCK
    xs = (_query_blocks(q, DSA_BLOCK), _query_blocks(q_idx, DSA_BLOCK),
          _query_blocks(w_idx, DSA_BLOCK), jnp.arange(nb) * DSA_BLOCK)
    return _unblock(lax.map(block, xs))


def diff_attention(q, k, v, lam, rel):
    S = q.shape[1]
    key_pos = jnp.arange(S)
    scale = DIFF_QK ** -0.5

    def block(args):
        qb, start = args
        qpos = start + jnp.arange(Q_BLOCK)
        dist = qpos[:, None] - key_pos[None, :]
        bias = jnp.transpose(rel[t5_bucket(dist)].astype(jnp.float32), (2, 0, 1))
        logits = jnp.einsum('bqhcd,bshcd->bhcqs', qb, k).astype(jnp.float32) * scale + bias[None, :, None]
        logits = jnp.where(dist >= 0, logits, -jnp.inf)
        p = jax.nn.softmax(logits, axis=-1)
        attn = (p[:, :, 0] - lam * p[:, :, 1]).astype(v.dtype)
        return jnp.einsum('bhqs,bshd->bqhd', attn, v)

    nb = S // Q_BLOCK
    return _unblock(lax.map(block, (_query_blocks(q, Q_BLOCK), jnp.arange(nb) * Q_BLOCK)))


def banded_attention(q, k, v, rel, dil, n_steps):
    N, n, H, hd = q.shape
    nb = -(-n // BAND)
    pad = nb * BAND - n
    to_blocks = lambda a: jnp.pad(a, ((0, 0), (0, pad), (0, 0), (0, 0))).reshape(N, nb, BAND, H, hd)
    qb, kb, vb = to_blocks(q), to_blocks(k), to_blocks(v)
    prev = lambda a: jnp.concatenate([jnp.zeros_like(a[:, :1]), a[:, :-1]], axis=1)
    kk = jnp.concatenate([prev(kb), kb], axis=2)
    vv = jnp.concatenate([prev(vb), vb], axis=2)
    m = BAND + jnp.arange(BAND)[:, None] - jnp.arange(2 * BAND)[None, :]
    first = (jnp.arange(nb)[:, None] == 0) & (jnp.arange(2 * BAND)[None, :] < BAND)
    mask = ((m >= 0) & (m <= n_steps))[None] & ~first[:, None, :]
    bias = jnp.transpose(rel[t5_bucket(m * dil)].astype(jnp.float32), (2, 0, 1))
    logits = jnp.einsum('nbqhd,nbkhd->nbhqk', qb, kk).astype(jnp.float32) * hd ** -0.5 + bias[None, None]
    logits = jnp.where(mask[None, :, None], logits, -jnp.inf)
    mx = logits.max(-1, keepdims=True)
    e = jnp.exp(logits - mx)
    den = e.sum(-1, keepdims=True)
    out = jnp.einsum('nbhqk,nbkhd->nbqhd', (e / den).astype(v.dtype), vv).reshape(N, nb * BAND, H, hd)[:, :n]
    lse = jnp.swapaxes((mx + jnp.log(den))[..., 0], 2, 3).reshape(N, nb * BAND, H)[:, :n]
    return out, lse


def dilated_attention(q, k, v, rel):
    B, S, H, hd = q.shape
    outs, lses = [], []
    for window, dil in DIL_PATTERNS:
        n = S // dil
        to_res = lambda a: jnp.swapaxes(a.reshape((B, n, dil) + a.shape[2:]), 1, 2).reshape((B * dil, n) + a.shape[2:])
        o, l = banded_attention(to_res(q), to_res(k), to_res(v), rel, dil, window // dil)
        outs.append(jnp.swapaxes(o.reshape(B, dil, n, H, hd), 1, 2).reshape(B, S, H, hd))
        lses.append(jnp.swapaxes(l.reshape(B, dil, n, H), 1, 2).reshape(B, S, H))
    wts = jax.nn.softmax(jnp.stack(lses, 0), axis=0)
    return jnp.einsum('pbsh,pbshd->bshd', wts.astype(q.dtype), jnp.stack(outs, 0))


def causal_attention(q, k, v, scale):
    S = q.shape[1]
    key_pos = jnp.arange(S)

    def block(args):
        qb, start = args
        qpos = start + jnp.arange(Q_BLOCK)
        logits = jnp.einsum('bqhd,bshd->bhqs', qb, k).astype(jnp.float32) * scale
        logits = jnp.where(key_pos[None, :] <= qpos[:, None], logits, -jnp.inf)
        p = jax.nn.softmax(logits, axis=-1).astype(v.dtype)
        return jnp.einsum('bhqs,bshd->bqhd', p, v)

    nb = S // Q_BLOCK
    return _unblock(lax.map(block, (_query_blocks(q, Q_BLOCK), jnp.arange(nb) * Q_BLOCK)))


def moe_ffn(x, w_router, b_router, w_gu, b_gu, w_dn, b_dn):
    B, S, D = x.shape
    T = B * S
    xt = x.reshape(T, D)
    logits = jnp.dot(xt, w_router).astype(jnp.float32) + b_router.astype(jnp.float32)
    top_vals, top_idx = lax.top_k(logits, TOP_K)
    gates = jax.nn.softmax(top_vals, axis=-1)
    N = T * TOP_K
    flat_e = top_idx.reshape(N)
    order = jnp.argsort(flat_e)
    sorted_e = flat_e[order]
    counts = jnp.bincount(flat_e, length=N_EXPERTS)
    padded = (counts + MOE_BLOCK - 1) // MOE_BLOCK * MOE_BLOCK
    pad_end = jnp.cumsum(padded)
    pad_start = pad_end - padded
    grp_start = jnp.cumsum(counts) - counts
    dest = pad_start[sorted_e] + jnp.arange(N) - grp_start[sorted_e]
    n_blocks = -(-N // MOE_BLOCK) + N_EXPERTS
    P = n_blocks * MOE_BLOCK
    row_tok = jnp.zeros((P,), jnp.int32).at[dest].set((order // TOP_K).astype(jnp.int32))
    block_e = jnp.minimum(jnp.searchsorted(pad_end, jnp.arange(n_blocks) * MOE_BLOCK, side='right'), N_EXPERTS - 1)

    def expert_block(args):
        tok, e = args
        h = jnp.dot(xt[tok], w_gu[e]) + b_gu[e]
        glu = jnp.minimum(h[:, :D_EXPERT], SWIGLU_LIMIT)
        lin = jnp.clip(h[:, D_EXPERT:], -SWIGLU_LIMIT, SWIGLU_LIMIT)
        a = glu * jax.nn.sigmoid(SWIGLU_ALPHA * glu) * (lin + 1)
        return jnp.dot(a, w_dn[e]) + b_dn[e]

    y = lax.map(expert_block, (row_tok.reshape(n_blocks, MOE_BLOCK), block_e)).reshape(P, D)
    slot = jnp.zeros((N,), jnp.int32).at[order].set(dest).reshape(T, TOP_K)
    gates = gates.astype(y.dtype)
    out = gates[:, 0:1] * y[slot[:, 0]]
    for j in range(1, TOP_K):
        out = out + gates[:, j:j + 1] * y[slot[:, j]]
    return out.reshape(B, S, D)


def setup_inputs(seed: int = 0) -> dict:
    key = jax.random.key(seed)
    ks = jax.random.split(key, 24)
    nrm = lambda k, shape, scale: jax.random.normal(k, shape, jnp.float32) * scale
    start = jax.random.randint(ks[1], (BATCH, 1), 0, 4096, dtype=jnp.int32)
    return {
        'x': nrm(ks[0], (BATCH, SEQ, D_MODEL), 1.0),
        'positions': start + jnp.arange(SEQ, dtype=jnp.int32)[None, :],
        'rel_bias': nrm(ks[2], (T5_BUCKETS, N_BIAS_HEADS), 0.5),
        'w_in': nrm(ks[3], (DEPTH, D_MODEL, D_IN), D_MODEL ** -0.5),
        'w_o': nrm(ks[4], (DEPTH, MIX_WIDTH, D_MODEL), MIX_WIDTH ** -0.5 * DEEPNORM_BETA),
        'ln1_g': 1.0 + nrm(ks[5], (DEPTH, D_MODEL), 0.02),
        'ln1_b': nrm(ks[6], (DEPTH, D_MODEL), 0.02),
        'ln2_g': 1.0 + nrm(ks[7], (DEPTH, D_MODEL), 0.02),
        'ln2_b': nrm(ks[8], (DEPTH, D_MODEL), 0.02),
        'mla_g_q': 1.0 + nrm(ks[9], (DEPTH, MLA_Q_RANK), 0.02),
        'mla_w_uq': nrm(ks[10], (DEPTH, MLA_Q_RANK, MLA_HEADS * (MLA_NOPE + MLA_ROPE)), MLA_Q_RANK ** -0.5),
        'mla_g_kv': 1.0 + nrm(ks[11], (DEPTH, MLA_KV_RANK), 0.02),
        'mla_w_ukv': nrm(ks[12], (DEPTH, MLA_KV_RANK, MLA_HEADS * (MLA_NOPE + MLA_V)), MLA_KV_RANK ** -0.5),
        'diff_lq1': nrm(ks[13], (DEPTH, DIFF_QK), 0.1),
        'diff_lk1': nrm(ks[14], (DEPTH, DIFF_QK), 0.1),
        'diff_lq2': nrm(ks[15], (DEPTH, DIFF_QK), 0.1),
        'diff_lk2': nrm(ks[16], (DEPTH, DIFF_QK), 0.1),
        'diff_subln_g': 1.0 + nrm(ks[17], (DEPTH, HEAD_DIM), 0.02),
        'w_router': nrm(ks[18], (DEPTH, D_MODEL, N_EXPERTS), D_MODEL ** -0.5),
        'b_router': nrm(ks[19], (DEPTH, N_EXPERTS), 0.01),
        'w_gate_up': nrm(ks[20], (DEPTH, N_EXPERTS, D_MODEL, 2 * D_EXPERT), D_MODEL ** -0.5),
        'b_gate_up': nrm(ks[21], (DEPTH, N_EXPERTS, 2 * D_EXPERT), 0.01),
        'w_down': nrm(ks[22], (DEPTH, N_EXPERTS, D_EXPERT, D_MODEL), D_EXPERT ** -0.5 * DEEPNORM_BETA),
        'b_down': nrm(ks[23], (DEPTH, N_EXPERTS, D_MODEL), 0.01),
    }


def reference(x, positions, rel_bias, w_in, w_o, ln1_g, ln1_b, ln2_g, ln2_b,
              mla_g_q, mla_w_uq, mla_g_kv, mla_w_ukv,
              diff_lq1, diff_lk1, diff_lq2, diff_lk2, diff_subln_g,
              w_router, b_router, w_gate_up, b_gate_up, w_down, b_down):
    B, S, _ = x.shape
    rel_a = rel_bias[:, :DSA_HEADS]
    rel_b = rel_bias[:, DSA_HEADS:DSA_HEADS + DIFF_HEADS]
    rel_c = rel_bias[:, DSA_HEADS + DIFF_HEADS:]
    offsets = [int(o) for o in np.cumsum(IN_SPLITS)[:-1]]
    heads = lambda a, h: a.reshape(B, S, h, a.shape[-1] // h)
    for l in range(DEPTH):
        u = jnp.einsum('bsd,de->bse', x, w_in[l])
        (qa, ka, va, qi, ki, wi, qb, kb, vb, qc, kc, vc, cq, ckv, kr) = jnp.split(u, offsets, axis=-1)

        qi = heads(qi, IDX_HEADS)
        qi = jnp.concatenate([rope(qi[..., :IDX_ROPE], positions), qi[..., IDX_ROPE:]], -1)
        ki = jnp.concatenate([rope(ki[..., :IDX_ROPE], positions), ki[..., IDX_ROPE:]], -1)
        ya = dsa_attention(heads(qa, DSA_HEADS), heads(ka, DSA_HEADS), heads(va, DSA_HEADS), qi, ki, wi, rel_a)

        lam_init = 0.8 - 0.6 * math.exp(-0.3 * l)
        lam = (jnp.exp(jnp.sum(diff_lq1[l].astype(jnp.float32) * diff_lk1[l].astype(jnp.float32)))
               - jnp.exp(jnp.sum(diff_lq2[l].astype(jnp.float32) * diff_lk2[l].astype(jnp.float32))) + lam_init)
        yb = diff_attention(qb.reshape(B, S, DIFF_HEADS, 2, DIFF_QK), kb.reshape(B, S, DIFF_HEADS, 2, DIFF_QK),
                            heads(vb, DIFF_HEADS), lam, rel_b)
        yb = rms_norm(yb, diff_subln_g[l]) * (1.0 - lam_init)

        yc = dilated_attention(heads(qc, DIL_HEADS), heads(kc, DIL_HEADS), heads(vc, DIL_HEADS), rel_c)

        q = jnp.dot(rms_norm(cq, mla_g_q[l]), mla_w_uq[l]).reshape(B, S, MLA_HEADS, MLA_NOPE + MLA_ROPE)
        q = jnp.concatenate([q[..., :MLA_NOPE], rope(q[..., MLA_NOPE:], positions)], -1)
        kv = jnp.dot(rms_norm(ckv, mla_g_kv[l]), mla_w_ukv[l]).reshape(B, S, MLA_HEADS, MLA_NOPE + MLA_V)
        k_rope = jnp.broadcast_to(rope(kr, positions)[:, :, None, :], (B, S, MLA_HEADS, MLA_ROPE))
        k = jnp.concatenate([kv[..., :MLA_NOPE], k_rope], -1)
        yd = causal_attention(q, k, kv[..., MLA_NOPE:], (MLA_NOPE + MLA_ROPE) ** -0.5)

        mix = jnp.concatenate([ya, yb, yc, yd], axis=2).reshape(B, S, MIX_WIDTH)
        x = layer_norm(DEEPNORM_ALPHA * x + jnp.einsum('bse,ed->bsd', mix, w_o[l]), ln1_g[l], ln1_b[l])

        m = moe_ffn(x, w_router[l], b_router[l], w_gate_up[l], b_gate_up[l], w_down[l], b_down[l])
        x = layer_norm(DEEPNORM_ALPHA * x + m, ln2_g[l], ln2_b[l])
    return x
```

```python
import functools
import math

import jax
import jax.numpy as jnp
import numpy as np
from jax import lax
from jax.experimental import pallas as pl
from jax.experimental.pallas import tpu as pltpu

D_MODEL = 2048
HEAD_DIM = 128
N_HEADS = 4
GROUP_W = N_HEADS * HEAD_DIM
IDX_HEADS = 16
IDX_DIM = 64
IDX_ROPE = 32
DSA_TOPK = 256
DIFF_QK = 64
DIL_PATTERNS = ((128, 1), (512, 4), (2048, 16))
BAND = 128
MLA_Q_RANK = 384
MLA_KV_RANK = 256
MLA_NOPE = 128
MLA_ROPE = 64
T5_BUCKETS = 32
T5_MAX_DIST = 2048
ROPE_THETA = 10000.0
N_EXPERTS = 32
TOP_K = 4
D_EXPERT = 768
SWIGLU_ALPHA = 1.702
SWIGLU_LIMIT = 7.0
EPS = 1e-5

LANES = 128
ATT_BLK = 256
MOE_BLK = 512
NEG = -1e30
INT_MIN = -2 ** 31
VMEM_LIMIT = 56 * 1024 * 1024

IDX_W = IDX_HEADS * IDX_DIM + LANES
MLA_W = 768
MLA_QK = 256


def _cparams(sem):
    return pltpu.CompilerParams(dimension_semantics=sem, vmem_limit_bytes=VMEM_LIMIT)


def _matmul_body(x_ref, w_ref, o_ref):
    o_ref[...] = jnp.dot(x_ref[...], w_ref[...], preferred_element_type=jnp.float32).astype(o_ref.dtype)


def _matmul(x, w, out_dtype, tm, tn):
    M, K = x.shape
    N = w.shape[1]
    return pl.pallas_call(
        _matmul_body,
        grid=(N // tn, M // tm),
        in_specs=[pl.BlockSpec((tm, K), lambda n, m: (m, 0)),
                  pl.BlockSpec((K, tn), lambda n, m: (0, n))],
        out_specs=pl.BlockSpec((tm, tn), lambda n, m: (m, n)),
        out_shape=jax.ShapeDtypeStruct((M, N), out_dtype),
        compiler_params=_cparams(("arbitrary", "arbitrary")),
        name="proj_matmul",
    )(x, w)


def _rot_partner(x, half):
    width = x.shape[-1]
    lane = lax.broadcasted_iota(jnp.int32, x.shape, x.ndim - 1)
    fwd = pltpu.roll(x, width - half, x.ndim - 1)
    bwd = pltpu.roll(x, half, x.ndim - 1)
    return jnp.where((lane % (2 * half)) < half, fwd, bwd)


def _rope_tables(positions, half, period):
    pos = positions.reshape(-1).astype(jnp.float32)
    inv = ROPE_THETA ** (-jnp.arange(half, dtype=jnp.float32) / half)
    ang = pos[:, None] * inv
    cos, sin = jnp.cos(ang), jnp.sin(ang)
    rest = period - 2 * half
    ones = jnp.ones((pos.shape[0], rest), jnp.float32)
    zeros = jnp.zeros((pos.shape[0], rest), jnp.float32)
    c = jnp.concatenate([cos, cos, ones], axis=1)
    s = jnp.concatenate([-sin, sin, zeros], axis=1)
    reps = LANES // period
    return jnp.tile(c, (1, reps)), jnp.tile(s, (1, reps))


def _idx_prep_body(u_ref, cos_ref, sin_ref, qi_ref, kie_ref, kio_ref, w_ref):
    cos = cos_ref[...]
    sin = sin_ref[...]
    half = IDX_ROPE // 2
    for c in range(IDX_HEADS * IDX_DIM // LANES):
        x = u_ref[:, c * LANES:(c + 1) * LANES]
        qi_ref[:, c * LANES:(c + 1) * LANES] = (x * cos + _rot_partner(x, half) * sin).astype(qi_ref.dtype)
    tail = u_ref[:, IDX_HEADS * IDX_DIM:]
    lane = lax.broadcasted_iota(jnp.int32, tail.shape, 1)
    roped = tail * cos + _rot_partner(tail, half) * sin
    even = jnp.where(lane < IDX_DIM, roped, 0.0)
    kie_ref[...] = even.astype(kie_ref.dtype)
    kio_ref[...] = pltpu.roll(even, IDX_DIM, 1).astype(kio_ref.dtype)
    w_ref[...] = tail * (IDX_HEADS ** -0.5 * IDX_DIM ** -0.5)


def _idx_prep(u_idx, cos_i, sin_i, tm=512):
    T = u_idx.shape[0]
    row = lambda w: pl.BlockSpec((tm, w), lambda i: (i, 0))
    return pl.pallas_call(
        _idx_prep_body,
        grid=(T // tm,),
        in_specs=[row(IDX_W), row(LANES), row(LANES)],
        out_specs=[row(IDX_HEADS * IDX_DIM), row(LANES), row(LANES), row(LANES)],
        out_shape=[jax.ShapeDtypeStruct((T, IDX_HEADS * IDX_DIM), jnp.bfloat16),
                   jax.ShapeDtypeStruct((T, LANES), jnp.bfloat16),
                   jax.ShapeDtypeStruct((T, LANES), jnp.bfloat16),
                   jax.ShapeDtypeStruct((T, LANES), jnp.float32)],
        compiler_params=_cparams(("arbitrary",)),
        name="idx_prep",
    )(u_idx, cos_i, sin_i)


def _nt_dot(a, b):
    return lax.dot_general(a, b, (((1,), (1,)), ((), ())), preferred_element_type=jnp.float32)


def _online_step(s, v_tile, carry):
    m, l, acc = carry
    m_new = jnp.maximum(m, jnp.max(s, axis=1, keepdims=True))
    alpha = jnp.exp(m - m_new)
    p = jnp.exp(s - m_new)
    l_new = alpha * l + jnp.sum(p, axis=1, keepdims=True)
    acc_new = alpha * acc + jnp.dot(p.astype(v_tile.dtype), v_tile, preferred_element_type=jnp.float32)
    return m_new, l_new, acc_new


def _softmax_init(rows, dv):
    return (jnp.full((rows, 1), NEG, jnp.float32), jnp.zeros((rows, 1), jnp.float32),
            jnp.zeros((rows, dv), jnp.float32))


def _causal_keep(j, kb):
    row = lax.broadcasted_iota(jnp.int32, (ATT_BLK, ATT_BLK), 0)
    col = lax.broadcasted_iota(jnp.int32, (ATT_BLK, ATT_BLK), 1)
    return (col - row) <= (j - kb) * ATT_BLK


def _float_key(score):
    bits = lax.bitcast_convert_type(score + 0.0, jnp.int32)
    return bits ^ (lax.shift_right_arithmetic(bits, 31) & jnp.int32(0x7FFFFFFF))


def _dsa_body(qi_ref, kie_ref, kio_ref, w_ref, q_ref, k_ref, v_ref, bias_ref, o_ref, key_scr, mask_scr, *, n_sel):
    j = pl.program_id(1)
    n_tiles = j + 1
    w = w_ref[...]

    def score_tile(kb, _):
        ks = pl.multiple_of(kb * ATT_BLK, ATT_BLK)
        kie = kie_ref[pl.ds(ks, ATT_BLK), :]
        kio = kio_ref[pl.ds(ks, ATT_BLK), :]
        score = jnp.zeros((ATT_BLK, ATT_BLK), jnp.float32)
        for h in range(IDX_HEADS):
            pair = qi_ref[:, (h // 2) * LANES:(h // 2 + 1) * LANES]
            dots = _nt_dot(pair, kie if h % 2 == 0 else kio)
            score = score + jnp.maximum(dots, 0.0) * w[:, IDX_DIM + h:IDX_DIM + h + 1]
        key = jnp.where(_causal_keep(j, kb), _float_key(score), jnp.int32(INT_MIN))
        key_scr[:, pl.ds(ks, ATT_BLK)] = key
        return 0

    lax.fori_loop(0, n_tiles, score_tile, 0)

    def count_ge(cand):
        def body(kb, cnt):
            ks = pl.multiple_of(kb * ATT_BLK, ATT_BLK)
            hit = key_scr[:, pl.ds(ks, ATT_BLK)] >= cand
            return cnt + jnp.sum(hit.astype(jnp.float32), axis=1, keepdims=True)
        return lax.fori_loop(0, n_tiles, body, jnp.zeros((ATT_BLK, 1), jnp.float32))

    zero = jnp.zeros((ATT_BLK, 1), jnp.int32)
    thr = jnp.where(count_ge(zero) >= n_sel, zero, jnp.int32(INT_MIN))

    def bit_step(i, thr):
        cand = thr | lax.shift_left(jnp.int32(1), jnp.int32(30) - i)
        return jnp.where(count_ge(cand) >= n_sel, cand, thr)

    thr = lax.fori_loop(0, 31, bit_step, thr)

    def mask_tile(kb, _):
        ks = pl.multiple_of(kb * ATT_BLK, ATT_BLK)
        key = key_scr[:, pl.ds(ks, ATT_BLK)]
        sel = (key >= thr) & (key != jnp.int32(INT_MIN))
        mask_scr[:, pl.ds(ks, ATT_BLK)] = jnp.where(sel, 0.0, NEG)
        return 0

    lax.fori_loop(0, n_tiles, mask_tile, 0)

    scale = HEAD_DIM ** -0.5
    n_bias_tiles = bias_ref.shape[2] // ATT_BLK
    for h in range(N_HEADS):
        cols = slice(h * HEAD_DIM, (h + 1) * HEAD_DIM)
        q = q_ref[:, cols]

        def kv_step(kb, carry, cols=cols, q=q, h=h):
            ks = pl.multiple_of(kb * ATT_BLK, ATT_BLK)
            bs = pl.multiple_of((kb + n_bias_tiles - 1 - j) * ATT_BLK, ATT_BLK)
            s = _nt_dot(q, k_ref[pl.ds(ks, ATT_BLK), cols]) * scale
            s = s + bias_ref[h, :, pl.ds(bs, ATT_BLK)] + mask_scr[:, pl.ds(ks, ATT_BLK)]
            return _online_step(s, v_ref[pl.ds(ks, ATT_BLK), cols], carry)

        _, l, acc = lax.fori_loop(0, n_tiles, kv_step, _softmax_init(ATT_BLK, HEAD_DIM))
        o_ref[:, cols] = (acc / l).astype(o_ref.dtype)


def _dsa_attention(qi, kie, kio, w_idx, qkv, bias_u, B, S):
    nq = S // ATT_BLK
    n_sel = min(DSA_TOPK, S // 4)
    qrow = lambda w, c: pl.BlockSpec((ATT_BLK, w), lambda b, j: (b * nq + j, c))
    seq = lambda w, c: pl.BlockSpec((S, w), lambda b, j: (b, c))
    return pl.pallas_call(
        functools.partial(_dsa_body, n_sel=n_sel),
        grid=(B, nq),
        in_specs=[qrow(IDX_HEADS * IDX_DIM, 0), seq(LANES, 0), seq(LANES, 0), qrow(LANES, 0),
                  qrow(GROUP_W, 0), seq(GROUP_W, 1), seq(GROUP_W, 2),
                  pl.BlockSpec(bias_u.shape, lambda b, j: (0, 0, 0))],
        out_specs=qrow(GROUP_W, 0),
        out_shape=jax.ShapeDtypeStruct((B * S, GROUP_W), jnp.bfloat16),
        scratch_shapes=[pltpu.VMEM((ATT_BLK, S), jnp.int32), pltpu.VMEM((ATT_BLK, S), jnp.float32)],
        compiler_params=_cparams(("arbitrary", "arbitrary")),
        name="dsa_attention",
    )(qi, kie, kio, w_idx, qkv, qkv, qkv, bias_u)


def _diff_body(lam_ref, q_ref, k_ref, v_ref, bias_ref, g_ref, o_ref, *, out_scale):
    j = pl.program_id(1)
    lam = lam_ref[0]
    scale = DIFF_QK ** -0.5
    n_bias_tiles = bias_ref.shape[2] // ATT_BLK
    lane = lax.broadcasted_iota(jnp.int32, (ATT_BLK, HEAD_DIM), 1)
    for h in range(N_HEADS):
        cols = slice(h * HEAD_DIM, (h + 1) * HEAD_DIM)
        q = q_ref[:, cols]

        def kv_step(kb, carry, cols=cols, q=q, h=h):
            c1, c2 = carry
            ks = pl.multiple_of(kb * ATT_BLK, ATT_BLK)
            bs = pl.multiple_of((kb + n_bias_tiles - 1 - j) * ATT_BLK, ATT_BLK)
            k = k_ref[pl.ds(ks, ATT_BLK), cols]
            v = v_ref[pl.ds(ks, ATT_BLK), cols]
            extra = bias_ref[h, :, pl.ds(bs, ATT_BLK)] + jnp.where(_causal_keep(j, kb), 0.0, NEG)
            s1 = _nt_dot(q, jnp.where(lane < DIFF_QK, k, jnp.zeros_like(k))) * scale + extra
            s2 = _nt_dot(q, jnp.where(lane >= DIFF_QK, k, jnp.zeros_like(k))) * scale + extra
            return _online_step(s1, v, c1), _online_step(s2, v, c2)

        init = (_softmax_init(ATT_BLK, HEAD_DIM), _softmax_init(ATT_BLK, HEAD_DIM))
        (_, l1, a1), (_, l2, a2) = lax.fori_loop(0, j + 1, kv_step, init)
        y = a1 / l1 - lam * (a2 / l2)
        y = y * lax.rsqrt(jnp.mean(jnp.square(y), axis=1, keepdims=True) + EPS) * g_ref[...]
        o_ref[:, cols] = (y * out_scale).astype(o_ref.dtype)


def _diff_attention(lam, qkv, bias_u, subln_g, lam_init, B, S):
    nq = S // ATT_BLK
    qrow = lambda c: pl.BlockSpec((ATT_BLK, GROUP_W), lambda b, j: (b * nq + j, c))
    seq = lambda c: pl.BlockSpec((S, GROUP_W), lambda b, j: (b, c))
    return pl.pallas_call(
        functools.partial(_diff_body, out_scale=1.0 - lam_init),
        grid=(B, nq),
        in_specs=[pl.BlockSpec(memory_space=pltpu.SMEM),
                  qrow(3), seq(4), seq(5),
                  pl.BlockSpec(bias_u.shape, lambda b, j: (0, 0, 0)),
                  pl.BlockSpec((1, HEAD_DIM), lambda b, j: (0, 0))],
        out_specs=qrow(0),
        out_shape=jax.ShapeDtypeStruct((B * S, GROUP_W), jnp.bfloat16),
        compiler_params=_cparams(("arbitrary", "arbitrary")),
        name="diff_attention",
    )(lam, qkv, qkv, qkv, bias_u, subln_g)


def _dil_body(q_ref, k_ref, v_ref, bias_ref, o_ref, out_scr, lse_scr):
    S = q_ref.shape[0]
    scale = HEAD_DIM ** -0.5
    col = lax.broadcasted_iota(jnp.int32, (BAND, 2 * BAND), 1)
    for p, (_, dil) in enumerate(DIL_PATTERNS):
        n_blocks = S // dil // BAND
        bias = bias_ref[p, 0]

        def band_block(it, _, p=p, dil=dil, bias=bias):
            r = it % dil
            blk = it // dil
            cur = r + dil * BAND * blk
            prev = r + dil * BAND * jnp.maximum(blk - 1, 0)
            rows = lambda ref, start: ref[pl.ds(start, BAND, stride=dil), :] if dil > 1 else ref[pl.ds(start, BAND), :]
            q = rows(q_ref, cur).astype(jnp.bfloat16)
            kk = jnp.concatenate([rows(k_ref, prev), rows(k_ref, cur)], axis=0).astype(jnp.bfloat16)
            vv = jnp.concatenate([rows(v_ref, prev), rows(v_ref, cur)], axis=0).astype(jnp.bfloat16)
            s = _nt_dot(q, kk) * scale + bias
            s = jnp.where((blk == 0) & (col < BAND), NEG, s)
            mx = jnp.max(s, axis=1, keepdims=True)
            e = jnp.exp(s - mx)
            den = jnp.sum(e, axis=1, keepdims=True)
            out = jnp.dot((e / den).astype(jnp.bfloat16), vv, preferred_element_type=jnp.float32)
            lse = jnp.broadcast_to(mx + jnp.log(den), (BAND, HEAD_DIM))
            if dil > 1:
                out_scr[p, pl.ds(cur, BAND, stride=dil), :] = out
                lse_scr[p, pl.ds(cur, BAND, stride=dil), :] = lse
            else:
                out_scr[p, pl.ds(cur, BAND), :] = out
                lse_scr[p, pl.ds(cur, BAND), :] = lse
            return 0

        lax.fori_loop(0, dil * n_blocks, band_block, 0)

    def combine(c, _):
        rs = pl.ds(pl.multiple_of(c * ATT_BLK, ATT_BLK), ATT_BLK)
        l0, l1, l2 = lse_scr[0, rs, :], lse_scr[1, rs, :], lse_scr[2, rs, :]
        mx = jnp.maximum(jnp.maximum(l0, l1), l2)
        w0, w1, w2 = jnp.exp(l0 - mx), jnp.exp(l1 - mx), jnp.exp(l2 - mx)
        tot = w0 + w1 + w2
        y = (w0 / tot) * out_scr[0, rs, :] + (w1 / tot) * out_scr[1, rs, :] + (w2 / tot) * out_scr[2, rs, :]
        o_ref[rs, :] = y.astype(o_ref.dtype)
        return 0

    lax.fori_loop(0, S // ATT_BLK, combine, 0)


def _dil_attention(qkv_c, bias_c, B, S):
    seq = lambda off: pl.BlockSpec((S, HEAD_DIM), lambda b, h: (b, off + h))
    return pl.pallas_call(
        _dil_body,
        grid=(B, N_HEADS),
        in_specs=[seq(0), seq(N_HEADS), seq(2 * N_HEADS),
                  pl.BlockSpec((len(DIL_PATTERNS), 1, BAND, 2 * BAND), lambda b, h: (0, h, 0, 0))],
        out_specs=seq(0),
        out_shape=jax.ShapeDtypeStruct((B * S, GROUP_W), jnp.bfloat16),
        scratch_shapes=[pltpu.VMEM((len(DIL_PATTERNS), S, HEAD_DIM), jnp.float32),
                        pltpu.VMEM((len(DIL_PATTERNS), S, HEAD_DIM), jnp.float32)],
        compiler_params=_cparams(("arbitrary", "arbitrary")),
        name="dilated_attention",
    )(qkv_c, qkv_c, qkv_c, bias_c)


def _rms(x, g):
    return x * lax.rsqrt(jnp.mean(jnp.square(x), axis=1, keepdims=True) + EPS) * g


def _mla_prep_body(u_ref, gq_ref, gkv_ref, wq_ref, wkv_ref, cos_ref, sin_ref, q_ref, k_ref, v_ref):
    cos = cos_ref[...]
    sin = sin_ref[...]
    half = MLA_ROPE // 2
    rope = lambda x: x * cos + _rot_partner(x, half) * sin
    cq = _rms(u_ref[:, :MLA_Q_RANK], gq_ref[...]).astype(jnp.bfloat16)
    ckv = _rms(u_ref[:, MLA_Q_RANK:MLA_Q_RANK + MLA_KV_RANK], gkv_ref[...]).astype(jnp.bfloat16)
    q = jnp.dot(cq, wq_ref[...], preferred_element_type=jnp.float32)
    kv = jnp.dot(ckv, wkv_ref[...], preferred_element_type=jnp.float32)
    k_rope = rope(u_ref[:, MLA_Q_RANK + MLA_KV_RANK:]).astype(k_ref.dtype)
    for h in range(N_HEADS):
        base = h * MLA_QK
        q_ref[:, base:base + MLA_NOPE] = q[:, base:base + MLA_NOPE].astype(q_ref.dtype)
        q_ref[:, base + MLA_NOPE:base + MLA_QK] = rope(q[:, base + MLA_NOPE:base + MLA_QK]).astype(q_ref.dtype)
        k_ref[:, base:base + MLA_NOPE] = kv[:, h * MLA_NOPE:(h + 1) * MLA_NOPE].astype(k_ref.dtype)
        k_ref[:, base + MLA_NOPE:base + MLA_QK] = k_rope
    v_ref[...] = kv[:, N_HEADS * MLA_NOPE:].astype(v_ref.dtype)


def _mla_prep(u_mla, g_q, g_kv, w_uq, w_ukv, cos_m, sin_m, tm=512):
    T = u_mla.shape[0]
    row = lambda w: pl.BlockSpec((tm, w), lambda i: (i, 0))
    full = lambda a: pl.BlockSpec(a.shape, lambda i: (0, 0))
    return pl.pallas_call(
        _mla_prep_body,
        grid=(T // tm,),
        in_specs=[row(MLA_W), full(g_q), full(g_kv), full(w_uq), full(w_ukv), row(LANES), row(LANES)],
        out_specs=[row(N_HEADS * MLA_QK), row(N_HEADS * MLA_QK), row(GROUP_W)],
        out_shape=[jax.ShapeDtypeStruct((T, N_HEADS * MLA_QK), jnp.bfloat16),
                   jax.ShapeDtypeStruct((T, N_HEADS * MLA_QK), jnp.bfloat16),
                   jax.ShapeDtypeStruct((T, GROUP_W), jnp.bfloat16)],
        compiler_params=_cparams(("arbitrary",)),
        name="mla_prep",
    )(u_mla, g_q, g_kv, w_uq, w_ukv, cos_m, sin_m)


def _mla_body(q_ref, k_ref, v_ref, o_ref):
    j = pl.program_id(1)
    scale = (MLA_NOPE + MLA_ROPE) ** -0.5
    for h in range(N_HEADS):
        qk_cols = slice(h * MLA_QK, (h + 1) * MLA_QK)
        v_cols = slice(h * HEAD_DIM, (h + 1) * HEAD_DIM)
        q = q_ref[:, qk_cols]

        def kv_step(kb, carry, qk_cols=qk_cols, v_cols=v_cols, q=q):
            ks = pl.multiple_of(kb * ATT_BLK, ATT_BLK)
            s = _nt_dot(q, k_ref[pl.ds(ks, ATT_BLK), qk_cols]) * scale
            s = jnp.where(_causal_keep(j, kb), s, NEG)
            return _online_step(s, v_ref[pl.ds(ks, ATT_BLK), v_cols], carry)

        _, l, acc = lax.fori_loop(0, j + 1, kv_step, _softmax_init(ATT_BLK, HEAD_DIM))
        o_ref[:, v_cols] = (acc / l).astype(o_ref.dtype)


def _mla_attention(q, k, v, B, S):
    nq = S // ATT_BLK
    return pl.pallas_call(
        _mla_body,
        grid=(B, nq),
        in_specs=[pl.BlockSpec((ATT_BLK, N_HEADS * MLA_QK), lambda b, j: (b * nq + j, 0)),
                  pl.BlockSpec((S, N_HEADS * MLA_QK), lambda b, j: (b, 0)),
                  pl.BlockSpec((S, GROUP_W), lambda b, j: (b, 0))],
        out_specs=pl.BlockSpec((ATT_BLK, GROUP_W), lambda b, j: (b * nq + j, 0)),
        out_shape=jax.ShapeDtypeStruct((B * S, GROUP_W), jnp.bfloat16),
        compiler_params=_cparams(("arbitrary", "arbitrary")),
        name="mla_attention",
    )(q, k, v)


def _layer_norm_rows(h, g, b):
    mu = jnp.mean(h, axis=1, keepdims=True)
    d = h - mu
    var = jnp.mean(jnp.square(d), axis=1, keepdims=True)
    return d * lax.rsqrt(var + EPS) * g + b


def _pack_bf16_pairs(x):
    n = x.shape[1] // 2
    lo = lax.bitcast_convert_type(x[:, :n].astype(jnp.bfloat16).astype(jnp.float32), jnp.int32)
    hi = lax.bitcast_convert_type(x[:, n:].astype(jnp.bfloat16).astype(jnp.float32), jnp.int32)
    return lax.shift_right_logical(lo, 16) | (hi & jnp.int32(-65536))


def _unpack_bf16_pairs(u):
    lo = lax.bitcast_convert_type(lax.shift_left(u, 16), jnp.float32).astype(jnp.bfloat16)
    hi = lax.bitcast_convert_type(u & jnp.int32(-65536), jnp.float32).astype(jnp.bfloat16)
    return lo, hi


def _wo_ln_body(ya_ref, yb_ref, yc_ref, yd_ref, wo_ref, x_ref, g_ref, b_ref, o_ref, op_ref, *, alpha):
    h = alpha * x_ref[...]
    for gi, y_ref in enumerate((ya_ref, yb_ref, yc_ref, yd_ref)):
        h = h + jnp.dot(y_ref[...], wo_ref[gi * GROUP_W:(gi + 1) * GROUP_W, :], preferred_element_type=jnp.float32)
    out = _layer_norm_rows(h, g_ref[...], b_ref[...])
    o_ref[...] = out
    op_ref[...] = _pack_bf16_pairs(out)


def _wo_ln(ys, w_o, x, g, b, alpha, tm=256):
    T = x.shape[0]
    row = lambda w: pl.BlockSpec((tm, w), lambda i: (i, 0))
    full = lambda a: pl.BlockSpec(a.shape, lambda i: (0, 0))
    return pl.pallas_call(
        functools.partial(_wo_ln_body, alpha=alpha),
        grid=(T // tm,),
        in_specs=[row(GROUP_W)] * 4 + [full(w_o), row(D_MODEL), full(g), full(b)],
        out_specs=[row(D_MODEL), row(D_MODEL // 2)],
        out_shape=[jax.ShapeDtypeStruct((T, D_MODEL), jnp.float32),
                   jax.ShapeDtypeStruct((T, D_MODEL // 2), jnp.int32)],
        compiler_params=_cparams(("arbitrary",)),
        name="wo_layernorm",
    )(*ys, w_o, x, g, b)


def _lane_slab(cols, dtype):
    rows = cols[0].shape[0]
    lane = lax.broadcasted_iota(jnp.int32, (rows, LANES), 1)
    out = jnp.zeros((rows, LANES), dtype)
    for i, c in enumerate(cols):
        out = jnp.where(lane == i, c.astype(dtype), out)
    return out


def _router_body(x_ref, w_ref, b_ref, idx_ref, gate_ref, rank_ref, cnt_ref, carry_scr):
    tm = x_ref.shape[0]

    @pl.when(pl.program_id(0) == 0)
    def _():
        carry_scr[...] = jnp.zeros_like(carry_scr)

    logits = jnp.dot(x_ref[...], w_ref[...], preferred_element_type=jnp.float32,
                     precision=lax.Precision.HIGHEST) + b_ref[...]
    lane = lax.broadcasted_iota(jnp.int32, logits.shape, 1)
    vals, idxs, hots = [], [], []
    for _ in range(TOP_K):
        m = jnp.max(logits, axis=1, keepdims=True)
        idx = jnp.min(jnp.where(logits == m, lane, N_EXPERTS), axis=1, keepdims=True)
        hot = lane == idx
        vals.append(m)
        idxs.append(idx)
        hots.append(hot)
        logits = jnp.where(hot, -jnp.inf, logits)
    exps = [jnp.exp(v - vals[0]) for v in vals]
    tot = exps[0] + exps[1] + exps[2] + exps[3]
    gates = [e / tot for e in exps]

    hot_f = [h.astype(jnp.float32) for h in hots]
    hot_all = hot_f[0] + hot_f[1] + hot_f[2] + hot_f[3]
    r = lax.broadcasted_iota(jnp.int32, (tm, tm), 0)
    c = lax.broadcasted_iota(jnp.int32, (tm, tm), 1)
    before = (c < r).astype(jnp.bfloat16)
    prior = jnp.dot(before, hot_all.astype(jnp.bfloat16), preferred_element_type=jnp.float32) + carry_scr[...]
    ranks = [jnp.sum(h * prior, axis=1, keepdims=True) for h in hot_f]
    carry_scr[...] = carry_scr[...] + jnp.sum(hot_all, axis=0, keepdims=True)

    idx_ref[...] = _lane_slab(idxs, jnp.int32)
    gate_ref[...] = _lane_slab(gates, jnp.float32)
    rank_ref[...] = _lane_slab(ranks, jnp.int32)
    cnt_ref[...] = carry_scr[...]


def _router(x, w_router, b_router, tm=512):
    T = x.shape[0]
    row = lambda w: pl.BlockSpec((tm, w), lambda i: (i, 0))
    full = lambda a: pl.BlockSpec(a.shape, lambda i: (0, 0))
    slab = lambda dt: jax.ShapeDtypeStruct((T, LANES), dt)
    return pl.pallas_call(
        _router_body,
        grid=(T // tm,),
        in_specs=[row(D_MODEL), full(w_router), full(b_router)],
        out_specs=[row(LANES), row(LANES), row(LANES), pl.BlockSpec((1, N_EXPERTS), lambda i: (0, 0))],
        out_shape=[slab(jnp.int32), slab(jnp.float32), slab(jnp.int32),
                   jax.ShapeDtypeStruct((1, N_EXPERTS), jnp.float32)],
        scratch_shapes=[pltpu.VMEM((1, N_EXPERTS), jnp.float32)],
        compiler_params=_cparams(("arbitrary",)),
        name="router",
    )(x, w_router, b_router)


def _dispatch_body(dest_ref, x_hbm, zero_hbm, o_hbm, sem, *, tm):
    del zero_hbm
    base = pl.program_id(0) * tm

    def copy(n):
        return pltpu.make_async_copy(x_hbm.at[pl.ds(base + n // TOP_K, 1)], o_hbm.at[pl.ds(dest_ref[n], 1)], sem)

    def start(n, _):
        copy(n).start()
        return 0

    def wait(n, _):
        copy(n).wait()
        return 0

    lax.fori_loop(0, tm * TOP_K, start, 0)
    lax.fori_loop(0, tm * TOP_K, wait, 0)


def _dispatch(dest_flat, x_packed, n_rows, tm=256):
    T, W = x_packed.shape
    zeros = jnp.zeros((n_rows, W), x_packed.dtype)
    return pl.pallas_call(
        functools.partial(_dispatch_body, tm=tm),
        grid=(T // tm,),
        in_specs=[pl.BlockSpec((tm * TOP_K,), lambda i: (i,), memory_space=pltpu.SMEM),
                  pl.BlockSpec(memory_space=pl.ANY), pl.BlockSpec(memory_space=pl.ANY)],
        out_specs=pl.BlockSpec(memory_space=pl.ANY),
        out_shape=jax.ShapeDtypeStruct((n_rows, W), x_packed.dtype),
        scratch_shapes=[pltpu.SemaphoreType.DMA(())],
        input_output_aliases={2: 0},
        compiler_params=_cparams(("arbitrary",)),
        name="moe_dispatch",
    )(dest_flat, x_packed, zeros)


def _ffn_body(be_ref, nu_ref, x_ref, wgu_ref, bgu_ref, wdn_ref, bdn_ref, y_ref):
    del be_ref

    @pl.when(pl.program_id(0) < nu_ref[0])
    def _():
        lo, hi = _unpack_bf16_pairs(x_ref[...])
        half = D_MODEL // 2
        h = (jnp.dot(lo, wgu_ref[0, :half, :], preferred_element_type=jnp.float32)
             + jnp.dot(hi, wgu_ref[0, half:, :], preferred_element_type=jnp.float32) + bgu_ref[0])
        glu = jnp.minimum(h[:, :D_EXPERT], SWIGLU_LIMIT)
        lin = jnp.clip(h[:, D_EXPERT:], -SWIGLU_LIMIT, SWIGLU_LIMIT)
        a = glu * jax.nn.sigmoid(SWIGLU_ALPHA * glu) * (lin + 1.0)
        y_ref[...] = jnp.dot(a.astype(jnp.bfloat16), wdn_ref[0], preferred_element_type=jnp.float32) + bdn_ref[0]

    @pl.when(pl.program_id(0) >= nu_ref[0])
    def _():
        y_ref[...] = jnp.zeros_like(y_ref)


def _expert_ffn(block_e, n_used, xg, w_gu, b_gu, w_dn, b_dn):
    P = xg.shape[0]
    n_blocks = P // MOE_BLK
    per_e = lambda a: pl.BlockSpec((1,) + a.shape[1:], lambda i, be, nu: (be[i], 0, 0))
    grid_spec = pltpu.PrefetchScalarGridSpec(
        num_scalar_prefetch=2,
        grid=(n_blocks,),
        in_specs=[pl.BlockSpec((MOE_BLK, D_MODEL // 2), lambda i, be, nu: (i, 0)),
                  per_e(w_gu), per_e(b_gu), per_e(w_dn), per_e(b_dn)],
        out_specs=pl.BlockSpec((MOE_BLK, D_MODEL), lambda i, be, nu: (i, 0)),
    )
    return pl.pallas_call(
        _ffn_body,
        grid_spec=grid_spec,
        out_shape=jax.ShapeDtypeStruct((P, D_MODEL), jnp.float32),
        compiler_params=_cparams(("arbitrary",)),
        name="expert_ffn",
    )(block_e, n_used, xg, w_gu, b_gu, w_dn, b_dn)


def _combine_body(dest_ref, y_hbm, x_ref, gate_ref, g_ref, b_ref, o_ref, ob_ref, buf, sem, *, alpha, tm):
    def copy(n):
        return pltpu.make_async_copy(y_hbm.at[pl.ds(dest_ref[n], 1)], buf.at[n % TOP_K, pl.ds(n // TOP_K, 1)], sem)

    def start(n, _):
        copy(n).start()
        return 0

    def wait(n, _):
        copy(n).wait()
        return 0

    lax.fori_loop(0, tm * TOP_K, start, 0)
    lax.fori_loop(0, tm * TOP_K, wait, 0)

    h = alpha * x_ref[...]
    gates = gate_ref[...]
    for jj in range(TOP_K):
        h = h + gates[:, jj:jj + 1] * buf[jj]
    out = _layer_norm_rows(h, g_ref[...], b_ref[...])
    o_ref[...] = out
    ob_ref[...] = out.astype(ob_ref.dtype)


def _combine(dest_flat, y, x, gates, g, b, alpha, tm=128):
    T = x.shape[0]
    row = lambda w: pl.BlockSpec((tm, w), lambda i: (i, 0))
    full = lambda a: pl.BlockSpec(a.shape, lambda i: (0, 0))
    return pl.pallas_call(
        functools.partial(_combine_body, alpha=alpha, tm=tm),
        grid=(T // tm,),
        in_specs=[pl.BlockSpec((tm * TOP_K,), lambda i: (i,), memory_space=pltpu.SMEM),
                  pl.BlockSpec(memory_space=pl.ANY), row(D_MODEL), row(LANES), full(g), full(b)],
        out_specs=[row(D_MODEL), row(D_MODEL)],
        out_shape=[jax.ShapeDtypeStruct((T, D_MODEL), jnp.float32),
                   jax.ShapeDtypeStruct((T, D_MODEL), jnp.bfloat16)],
        scratch_shapes=[pltpu.VMEM((TOP_K, tm, D_MODEL), jnp.float32), pltpu.SemaphoreType.DMA(())],
        compiler_params=_cparams(("arbitrary",)),
        name="moe_combine",
    )(dest_flat, y, x, gates, g, b)


def _t5_bucket(dist):
    n = jnp.maximum(dist, 0)
    exact = T5_BUCKETS // 2
    nf = jnp.maximum(n, 1).astype(jnp.float32)
    large = exact + (jnp.log(nf / exact) / math.log(T5_MAX_DIST / exact) * (T5_BUCKETS - exact)).astype(jnp.int32)
    return jnp.where(n < exact, n, jnp.minimum(large, T5_BUCKETS - 1))


def _toeplitz_bias(rel, S):
    i = jnp.arange(ATT_BLK)[:, None]
    m = jnp.arange(S)[None, :]
    dist = i - m + S - ATT_BLK
    tab = rel[_t5_bucket(dist)].astype(jnp.float32)
    return jnp.transpose(jnp.where((dist >= 0)[..., None], tab, 0.0), (2, 0, 1))


def _band_bias(rel):
    m = BAND + jnp.arange(BAND)[:, None] - jnp.arange(2 * BAND)[None, :]
    out = []
    for window, dil in DIL_PATTERNS:
        keep = (m >= 0) & (m <= window // dil)
        tab = jnp.transpose(rel[_t5_bucket(m * dil)].astype(jnp.float32), (2, 0, 1))
        out.append(jnp.where(keep[None], tab, NEG))
    return jnp.stack(out, 0)


def _layer_weights(w_in_l, w_uq_l, w_ukv_l):
    bf = jnp.bfloat16
    o = np.cumsum((0, 512, 512, 512, 1024, 64, 16, 512, 512, 512, 512, 512, 512, 384, 256, 64))
    seg = lambda a, b: w_in_l[:, int(o[a]):int(o[b])]
    w_ab = jnp.concatenate([seg(0, 3), seg(6, 9)], axis=1).astype(bf)
    w_c = seg(9, 12).astype(bf)
    pad = lambda n: jnp.zeros((D_MODEL, n), w_in_l.dtype)
    w_idx = jnp.concatenate([seg(3, 6), pad(IDX_W - 1104)], axis=1).astype(bf)
    w_mla = jnp.concatenate([seg(12, 15), pad(MLA_W - 704)], axis=1).astype(bf)
    uq = w_uq_l.reshape(MLA_Q_RANK, N_HEADS, MLA_NOPE + MLA_ROPE)
    uq = jnp.pad(uq, ((0, 0), (0, 0), (0, MLA_QK - MLA_NOPE - MLA_ROPE))).reshape(MLA_Q_RANK, N_HEADS * MLA_QK)
    ukv = w_ukv_l.reshape(MLA_KV_RANK, N_HEADS, 2, MLA_NOPE)
    ukv = jnp.transpose(ukv, (0, 2, 1, 3)).reshape(MLA_KV_RANK, 2 * N_HEADS * MLA_NOPE)
    return w_ab, w_c, w_idx, w_mla, uq.astype(bf), ukv.astype(bf)


def kernel(x, positions, rel_bias, w_in, w_o, ln1_g, ln1_b, ln2_g, ln2_b, mla_g_q, mla_w_uq, mla_g_kv, mla_w_ukv,
           diff_lq1, diff_lk1, diff_lq2, diff_lk2, diff_subln_g, w_router, b_router, w_gate_up, b_gate_up, w_down,
           b_down):
    B, S, D = x.shape
    depth = w_in.shape[0]
    T = B * S
    assert D == D_MODEL and S % (16 * BAND) == 0 and T % 1024 == 0
    alpha = (2 * depth) ** 0.25
    f32 = jnp.float32

    bias_a = _toeplitz_bias(rel_bias[:, :N_HEADS], S)
    bias_b = _toeplitz_bias(rel_bias[:, N_HEADS:2 * N_HEADS], S)
    bias_c = _band_bias(rel_bias[:, 2 * N_HEADS:])
    cos_i, sin_i = _rope_tables(positions, IDX_ROPE // 2, IDX_DIM)
    cos_m, sin_m = _rope_tables(positions, MLA_ROPE // 2, LANES)

    n_blocks = T * TOP_K // MOE_BLK + N_EXPERTS
    n_rows = n_blocks * MOE_BLK

    xf = x.reshape(T, D)
    xb = xf.astype(jnp.bfloat16)
    for l in range(depth):
        w_ab, w_c, w_idx, w_mla, w_uq, w_ukv = _layer_weights(w_in[l], mla_w_uq[l], mla_w_ukv[l])
        u_ab = _matmul(xb, w_ab, jnp.bfloat16, 1024, 512)
        u_c = _matmul(xb, w_c, f32, 1024, 512)
        u_idx = _matmul(xb, w_idx, f32, 1024, IDX_W // 3)
        u_mla = _matmul(xb, w_mla, f32, 1024, MLA_W)

        qi, kie, kio, w_i = _idx_prep(u_idx, cos_i, sin_i)
        ya = _dsa_attention(qi, kie, kio, w_i, u_ab, bias_a, B, S)

        lam_init = 0.8 - 0.6 * math.exp(-0.3 * l)
        lam = (jnp.exp(jnp.sum(diff_lq1[l].astype(f32) * diff_lk1[l].astype(f32)))
               - jnp.exp(jnp.sum(diff_lq2[l].astype(f32) * diff_lk2[l].astype(f32))) + lam_init).reshape(1)
        yb = _diff_attention(lam, u_ab, bias_b, diff_subln_g[l].reshape(1, HEAD_DIM).astype(f32), lam_init, B, S)

        yc = _dil_attention(u_c, bias_c, B, S)

        qd, kd, vd = _mla_prep(u_mla, mla_g_q[l].reshape(1, -1).astype(f32), mla_g_kv[l].reshape(1, -1).astype(f32),
                               w_uq, w_ukv, cos_m, sin_m)
        yd = _mla_attention(qd, kd, vd, B, S)

        x1, x1p = _wo_ln((ya, yb, yc, yd), w_o[l].astype(jnp.bfloat16), xf,
                         ln1_g[l].reshape(1, D).astype(f32), ln1_b[l].reshape(1, D).astype(f32), alpha)

        idx_s, gate_s, rank_s, counts = _router(x1, w_router[l].astype(f32), b_router[l].reshape(1, -1).astype(f32))
        counts = counts.reshape(N_EXPERTS).astype(jnp.int32)
        padded = (counts + MOE_BLK - 1) // MOE_BLK * MOE_BLK
        pad_end = jnp.cumsum(padded)
        pad_start = pad_end - padded
        dest = (pad_start[idx_s[:, :TOP_K]] + rank_s[:, :TOP_K]).reshape(T * TOP_K).astype(jnp.int32)
        block_e = jnp.minimum(jnp.searchsorted(pad_end, jnp.arange(n_blocks) * MOE_BLK, side='right'),
                              N_EXPERTS - 1).astype(jnp.int32)
        n_used = (pad_end[-1:] // MOE_BLK).astype(jnp.int32)

        xg = _dispatch(dest, x1p, n_rows)
        y = _expert_ffn(block_e, n_used, xg, w_gate_up[l].astype(jnp.bfloat16),
                        b_gate_up[l].reshape(N_EXPERTS, 1, -1).astype(f32), w_down[l].astype(jnp.bfloat16),
                        b_down[l].reshape(N_EXPERTS, 1, -1).astype(f32))
        xf, xb = _combine(dest, y, x1, gate_s, ln2_g[l].reshape(1, D).astype(f32),
                          ln2_b[l].reshape(1, D).astype(f32), alpha)
    return xf.reshape(B, S, D)
```

```python
import functools
import math

import jax
import jax.numpy as jnp
import numpy as np
from jax import lax
from jax.experimental import pallas as pl
from jax.experimental.pallas import tpu as pltpu

D_MODEL = 2048
HEAD_DIM = 128
N_HEADS = 4
GROUP_W = N_HEADS * HEAD_DIM
IDX_HEADS = 16
IDX_DIM = 64
IDX_ROPE = 32
DSA_TOPK = 256
DIFF_QK = 64
DIL_PATTERNS = ((128, 1), (512, 4), (2048, 16))
BAND = 128
MLA_Q_RANK = 384
MLA_KV_RANK = 256
MLA_NOPE = 128
MLA_ROPE = 64
T5_BUCKETS = 32
T5_MAX_DIST = 2048
ROPE_THETA = 10000.0
N_EXPERTS = 32
TOP_K = 4
D_EXPERT = 768
SWIGLU_ALPHA = 1.702
SWIGLU_LIMIT = 7.0
EPS = 1e-5

LANES = 128
ATT_BLK = 256
MOE_BLK = 512
NEG = -1e30
INT_MIN = -2 ** 31
VMEM_LIMIT = 56 * 1024 * 1024

IDX_W = IDX_HEADS * IDX_DIM + LANES
MLA_W = 768
MLA_QK = 256


def _cparams(sem):
    return pltpu.CompilerParams(dimension_semantics=sem, vmem_limit_bytes=VMEM_LIMIT)


def _matmul_body(x_ref, w_ref, o_ref):
    o_ref[...] = jnp.dot(x_ref[...], w_ref[...], preferred_element_type=jnp.float32).astype(o_ref.dtype)


def _matmul(x, w, out_dtype, tm, tn):
    M, K = x.shape
    N = w.shape[1]
    return pl.pallas_call(
        _matmul_body,
        grid=(N // tn, M // tm),
        in_specs=[pl.BlockSpec((tm, K), lambda n, m: (m, 0)),
                  pl.BlockSpec((K, tn), lambda n, m: (0, n))],
        out_specs=pl.BlockSpec((tm, tn), lambda n, m: (m, n)),
        out_shape=jax.ShapeDtypeStruct((M, N), out_dtype),
        compiler_params=_cparams(("arbitrary", "arbitrary")),
        name="proj_matmul",
    )(x, w)


def _rot_partner(x, half):
    width = x.shape[-1]
    lane = lax.broadcasted_iota(jnp.int32, x.shape, x.ndim - 1)
    fwd = pltpu.roll(x, width - half, x.ndim - 1)
    bwd = pltpu.roll(x, half, x.ndim - 1)
    return jnp.where((lane % (2 * half)) < half, fwd, bwd)


def _rope_tables(positions, half, period):
    pos = positions.reshape(-1).astype(jnp.float32)
    inv = ROPE_THETA ** (-jnp.arange(half, dtype=jnp.float32) / half)
    ang = pos[:, None] * inv
    cos, sin = jnp.cos(ang), jnp.sin(ang)
    rest = period - 2 * half
    ones = jnp.ones((pos.shape[0], rest), jnp.float32)
    zeros = jnp.zeros((pos.shape[0], rest), jnp.float32)
    c = jnp.concatenate([cos, cos, ones], axis=1)
    s = jnp.concatenate([-sin, sin, zeros], axis=1)
    reps = LANES // period
    return jnp.tile(c, (1, reps)), jnp.tile(s, (1, reps))


def _idx_prep_body(u_ref, cos_ref, sin_ref, qi_ref, kie_ref, kio_ref, w_ref):
    cos = cos_ref[...]
    sin = sin_ref[...]
    half = IDX_ROPE // 2
    for c in range(IDX_HEADS * IDX_DIM // LANES):
        x = u_ref[:, c * LANES:(c + 1) * LANES]
        qi_ref[:, c * LANES:(c + 1) * LANES] = (x * cos + _rot_partner(x, half) * sin).astype(qi_ref.dtype)
    tail = u_ref[:, IDX_HEADS * IDX_DIM:]
    lane = lax.broadcasted_iota(jnp.int32, tail.shape, 1)
    roped = tail * cos + _rot_partner(tail, half) * sin
    even = jnp.where(lane < IDX_DIM, roped, 0.0)
    kie_ref[...] = even.astype(kie_ref.dtype)
    kio_ref[...] = pltpu.roll(even, IDX_DIM, 1).astype(kio_ref.dtype)
    w_ref[...] = tail * (IDX_HEADS ** -0.5 * IDX_DIM ** -0.5)


def _idx_prep(u_idx, cos_i, sin_i, tm=512):
    T = u_idx.shape[0]
    row = lambda w: pl.BlockSpec((tm, w), lambda i: (i, 0))
    return pl.pallas_call(
        _idx_prep_body,
        grid=(T // tm,),
        in_specs=[row(IDX_W), row(LANES), row(LANES)],
        out_specs=[row(IDX_HEADS * IDX_DIM), row(LANES), row(LANES), row(LANES)],
        out_shape=[jax.ShapeDtypeStruct((T, IDX_HEADS * IDX_DIM), jnp.bfloat16),
                   jax.ShapeDtypeStruct((T, LANES), jnp.bfloat16),
                   jax.ShapeDtypeStruct((T, LANES), jnp.bfloat16),
                   jax.ShapeDtypeStruct((T, LANES), jnp.float32)],
        compiler_params=_cparams(("arbitrary",)),
        name="idx_prep",
    )(u_idx, cos_i, sin_i)


def _nt_dot(a, b):
    return lax.dot_general(a, b, (((1,), (1,)), ((), ())), preferred_element_type=jnp.float32)


def _online_step(s, v_tile, carry):
    m, l, acc = carry
    m_new = jnp.maximum(m, jnp.max(s, axis=1, keepdims=True))
    alpha = jnp.exp(m - m_new)
    p = jnp.exp(s - m_new)
    l_new = alpha * l + jnp.sum(p, axis=1, keepdims=True)
    acc_new = alpha * acc + jnp.dot(p.astype(v_tile.dtype), v_tile, preferred_element_type=jnp.float32)
    return m_new, l_new, acc_new


def _softmax_init(rows, dv):
    return (jnp.full((rows, 1), NEG, jnp.float32), jnp.zeros((rows, 1), jnp.float32),
            jnp.zeros((rows, dv), jnp.float32))


def _causal_keep(j, kb):
    row = lax.broadcasted_iota(jnp.int32, (ATT_BLK, ATT_BLK), 0)
    col = lax.broadcasted_iota(jnp.int32, (ATT_BLK, ATT_BLK), 1)
    return (col - row) <= (j - kb) * ATT_BLK


def _sweep_key_tiles(j, n_chains, logits_fn, value_fn):
    def step(kb, carries, diagonal):
        ks = pl.multiple_of(kb * ATT_BLK, ATT_BLK)
        out = []
        for c in range(n_chains):
            s = logits_fn(c, kb, ks)
            if diagonal:
                s = jnp.where(_causal_keep(0, 0), s, NEG)
            out.append(_online_step(s, value_fn(c, ks), carries[c]))
        return tuple(out)

    init = tuple(_softmax_init(ATT_BLK, HEAD_DIM) for _ in range(n_chains))
    carries = lax.fori_loop(0, j, functools.partial(step, diagonal=False), init)
    return step(j, carries, True)


def _float_key(score):
    bits = lax.bitcast_convert_type(score + 0.0, jnp.int32)
    return bits ^ (lax.shift_right_arithmetic(bits, 31) & jnp.int32(0x7FFFFFFF))


def _dsa_body(qi_ref, kie_ref, kio_ref, w_ref, q_ref, k_ref, v_ref, bias_ref, o_ref, key_scr, mask_scr, *, n_sel):
    j = pl.program_id(1)
    n_tiles = j + 1
    w = w_ref[...]

    def score_tile(kb, _):
        ks = pl.multiple_of(kb * ATT_BLK, ATT_BLK)
        kie = kie_ref[pl.ds(ks, ATT_BLK), :]
        kio = kio_ref[pl.ds(ks, ATT_BLK), :]
        score = jnp.zeros((ATT_BLK, ATT_BLK), jnp.float32)
        for h in range(IDX_HEADS):
            pair = qi_ref[:, (h // 2) * LANES:(h // 2 + 1) * LANES]
            dots = _nt_dot(pair, kie if h % 2 == 0 else kio)
            score = score + jnp.maximum(dots, 0.0) * w[:, IDX_DIM + h:IDX_DIM + h + 1]
        key = jnp.where(_causal_keep(j, kb), _float_key(score), jnp.int32(INT_MIN))
        key_scr[:, pl.ds(ks, ATT_BLK)] = key
        return 0

    lax.fori_loop(0, n_tiles, score_tile, 0)

    def count_ge(cand):
        def body(kb, cnt):
            ks = pl.multiple_of(kb * ATT_BLK, ATT_BLK)
            hit = jnp.where(key_scr[:, pl.ds(ks, ATT_BLK)] >= cand, 1.0, 0.0)
            return cnt + hit[:, :LANES] + hit[:, LANES:]
        cnt = lax.fori_loop(0, n_tiles, body, jnp.zeros((ATT_BLK, LANES), jnp.float32))
        return jnp.sum(cnt, axis=1, keepdims=True)

    zero = jnp.zeros((ATT_BLK, 1), jnp.int32)
    thr = jnp.where(count_ge(zero) >= n_sel, zero, jnp.int32(INT_MIN))

    def bit_step(i, thr):
        cand = thr | lax.shift_left(jnp.int32(1), jnp.int32(30) - i)
        return jnp.where(count_ge(cand) >= n_sel, cand, thr)

    thr = lax.fori_loop(0, 31, bit_step, thr)

    def mask_tile(kb, _):
        ks = pl.multiple_of(kb * ATT_BLK, ATT_BLK)
        key = key_scr[:, pl.ds(ks, ATT_BLK)]
        sel = (key >= thr) & (key != jnp.int32(INT_MIN))
        mask_scr[:, pl.ds(ks, ATT_BLK)] = jnp.where(sel, 0.0, NEG)
        return 0

    lax.fori_loop(0, n_tiles, mask_tile, 0)

    n_bias_tiles = bias_ref.shape[2] // ATT_BLK
    cols = [slice(h * HEAD_DIM, (h + 1) * HEAD_DIM) for h in range(N_HEADS)]
    qs = [q_ref[:, c] for c in cols]

    def logits(h, kb, ks):
        bs = pl.multiple_of((kb + n_bias_tiles - 1 - j) * ATT_BLK, ATT_BLK)
        s = _nt_dot(qs[h], k_ref[pl.ds(ks, ATT_BLK), cols[h]])
        return s + bias_ref[h, :, pl.ds(bs, ATT_BLK)] + mask_scr[:, pl.ds(ks, ATT_BLK)]

    done = _sweep_key_tiles(j, N_HEADS, logits, lambda h, ks: v_ref[pl.ds(ks, ATT_BLK), cols[h]])
    for h, (_, l, acc) in enumerate(done):
        o_ref[:, cols[h]] = (acc / l).astype(o_ref.dtype)


def _dsa_attention(qi, kie, kio, w_idx, qkv, bias_u, B, S):
    nq = S // ATT_BLK
    n_sel = min(DSA_TOPK, S // 4)
    qrow = lambda w, c: pl.BlockSpec((ATT_BLK, w), lambda b, j: (b * nq + j, c))
    seq = lambda w, c: pl.BlockSpec((S, w), lambda b, j: (b, c))
    return pl.pallas_call(
        functools.partial(_dsa_body, n_sel=n_sel),
        grid=(B, nq),
        in_specs=[qrow(IDX_HEADS * IDX_DIM, 0), seq(LANES, 0), seq(LANES, 0), qrow(LANES, 0),
                  qrow(GROUP_W, 0), seq(GROUP_W, 1), seq(GROUP_W, 2),
                  pl.BlockSpec(bias_u.shape, lambda b, j: (0, 0, 0))],
        out_specs=qrow(GROUP_W, 0),
        out_shape=jax.ShapeDtypeStruct((B * S, GROUP_W), jnp.bfloat16),
        scratch_shapes=[pltpu.VMEM((ATT_BLK, S), jnp.int32), pltpu.VMEM((ATT_BLK, S), jnp.float32)],
        compiler_params=_cparams(("arbitrary", "arbitrary")),
        name="dsa_attention",
    )(qi, kie, kio, w_idx, qkv, qkv, qkv, bias_u)


def _diff_body(lam_ref, q_ref, k_ref, v_ref, bias_ref, g_ref, o_ref, *, out_scale):
    j = pl.program_id(1)
    lam = lam_ref[0]
    n_bias_tiles = bias_ref.shape[2] // ATT_BLK
    lane = lax.broadcasted_iota(jnp.int32, (ATT_BLK, HEAD_DIM), 1)
    cols = [slice(h * HEAD_DIM, (h + 1) * HEAD_DIM) for h in range(N_HEADS)]
    qs = []
    for h in range(N_HEADS):
        q = q_ref[:, cols[h]]
        qs += [jnp.where(lane < DIFF_QK, q, jnp.zeros_like(q)), jnp.where(lane >= DIFF_QK, q, jnp.zeros_like(q))]

    def logits(c, kb, ks):
        bs = pl.multiple_of((kb + n_bias_tiles - 1 - j) * ATT_BLK, ATT_BLK)
        return _nt_dot(qs[c], k_ref[pl.ds(ks, ATT_BLK), cols[c // 2]]) + bias_ref[c // 2, :, pl.ds(bs, ATT_BLK)]

    done = _sweep_key_tiles(j, 2 * N_HEADS, logits, lambda c, ks: v_ref[pl.ds(ks, ATT_BLK), cols[c // 2]])
    for h in range(N_HEADS):
        (_, l1, a1), (_, l2, a2) = done[2 * h], done[2 * h + 1]
        y = a1 / l1 - lam * (a2 / l2)
        y = y * lax.rsqrt(jnp.mean(jnp.square(y), axis=1, keepdims=True) + EPS) * g_ref[...]
        o_ref[:, cols[h]] = (y * out_scale).astype(o_ref.dtype)


def _diff_attention(lam, qkv, bias_u, subln_g, lam_init, B, S):
    nq = S // ATT_BLK
    qrow = lambda c: pl.BlockSpec((ATT_BLK, GROUP_W), lambda b, j: (b * nq + j, c))
    seq = lambda c: pl.BlockSpec((S, GROUP_W), lambda b, j: (b, c))
    return pl.pallas_call(
        functools.partial(_diff_body, out_scale=1.0 - lam_init),
        grid=(B, nq),
        in_specs=[pl.BlockSpec(memory_space=pltpu.SMEM),
                  qrow(3), seq(4), seq(5),
                  pl.BlockSpec(bias_u.shape, lambda b, j: (0, 0, 0)),
                  pl.BlockSpec((1, HEAD_DIM), lambda b, j: (0, 0))],
        out_specs=qrow(0),
        out_shape=jax.ShapeDtypeStruct((B * S, GROUP_W), jnp.bfloat16),
        compiler_params=_cparams(("arbitrary", "arbitrary")),
        name="diff_attention",
    )(lam, qkv, qkv, qkv, bias_u, subln_g)


def _dil_body(q_ref, k_ref, v_ref, bias_ref, o_ref, out_scr, lse_scr):
    S = q_ref.shape[0]
    col = lax.broadcasted_iota(jnp.int32, (BAND, 2 * BAND), 1)
    for p, (_, dil) in enumerate(DIL_PATTERNS):
        n_blocks = S // dil // BAND
        bias = bias_ref[p, 0]

        def band_block(it, _, p=p, dil=dil, bias=bias):
            r = it % dil
            blk = it // dil
            cur = r + dil * BAND * blk
            prev = r + dil * BAND * jnp.maximum(blk - 1, 0)
            rows = lambda ref, start: ref[pl.ds(start, BAND, stride=dil), :] if dil > 1 else ref[pl.ds(start, BAND), :]
            q = rows(q_ref, cur).astype(jnp.bfloat16)
            kk = jnp.concatenate([rows(k_ref, prev), rows(k_ref, cur)], axis=0).astype(jnp.bfloat16)
            vv = jnp.concatenate([rows(v_ref, prev), rows(v_ref, cur)], axis=0).astype(jnp.bfloat16)
            s = _nt_dot(q, kk) + bias
            s = jnp.where((blk == 0) & (col < BAND), NEG, s)
            mx = jnp.max(s, axis=1, keepdims=True)
            e = jnp.exp(s - mx)
            den = jnp.sum(e, axis=1, keepdims=True)
            out = jnp.dot((e / den).astype(jnp.bfloat16), vv, preferred_element_type=jnp.float32)
            lse = jnp.broadcast_to(mx + jnp.log(den), (BAND, HEAD_DIM))
            if dil > 1:
                out_scr[p, pl.ds(cur, BAND, stride=dil), :] = out
                lse_scr[p, pl.ds(cur, BAND, stride=dil), :] = lse
            else:
                out_scr[p, pl.ds(cur, BAND), :] = out
                lse_scr[p, pl.ds(cur, BAND), :] = lse
            return 0

        lax.fori_loop(0, dil * n_blocks, band_block, 0)

    def combine(c, _):
        rs = pl.ds(pl.multiple_of(c * ATT_BLK, ATT_BLK), ATT_BLK)
        l0, l1, l2 = lse_scr[0, rs, :], lse_scr[1, rs, :], lse_scr[2, rs, :]
        mx = jnp.maximum(jnp.maximum(l0, l1), l2)
        w0, w1, w2 = jnp.exp(l0 - mx), jnp.exp(l1 - mx), jnp.exp(l2 - mx)
        tot = w0 + w1 + w2
        y = (w0 / tot) * out_scr[0, rs, :] + (w1 / tot) * out_scr[1, rs, :] + (w2 / tot) * out_scr[2, rs, :]
        o_ref[rs, :] = y.astype(o_ref.dtype)
        return 0

    lax.fori_loop(0, S // ATT_BLK, combine, 0)


def _dil_attention(qkv_c, bias_c, B, S):
    seq = lambda off: pl.BlockSpec((S, HEAD_DIM), lambda b, h: (b, off + h))
    return pl.pallas_call(
        _dil_body,
        grid=(B, N_HEADS),
        in_specs=[seq(0), seq(N_HEADS), seq(2 * N_HEADS),
                  pl.BlockSpec((len(DIL_PATTERNS), 1, BAND, 2 * BAND), lambda b, h: (0, h, 0, 0))],
        out_specs=seq(0),
        out_shape=jax.ShapeDtypeStruct((B * S, GROUP_W), jnp.bfloat16),
        scratch_shapes=[pltpu.VMEM((len(DIL_PATTERNS), S, HEAD_DIM), jnp.float32),
                        pltpu.VMEM((len(DIL_PATTERNS), S, HEAD_DIM), jnp.float32)],
        compiler_params=_cparams(("arbitrary", "arbitrary")),
        name="dilated_attention",
    )(qkv_c, qkv_c, qkv_c, bias_c)


def _rms(x, g):
    return x * lax.rsqrt(jnp.mean(jnp.square(x), axis=1, keepdims=True) + EPS) * g


def _mla_prep_body(u_ref, gq_ref, gkv_ref, wq_ref, wkv_ref, cos_ref, sin_ref, q_ref, k_ref, v_ref):
    cos = cos_ref[...]
    sin = sin_ref[...]
    half = MLA_ROPE // 2
    rope = lambda x: x * cos + _rot_partner(x, half) * sin
    cq = _rms(u_ref[:, :MLA_Q_RANK], gq_ref[...]).astype(jnp.bfloat16)
    ckv = _rms(u_ref[:, MLA_Q_RANK:MLA_Q_RANK + MLA_KV_RANK], gkv_ref[...]).astype(jnp.bfloat16)
    q = jnp.dot(cq, wq_ref[...], preferred_element_type=jnp.float32)
    kv = jnp.dot(ckv, wkv_ref[...], preferred_element_type=jnp.float32)
    k_rope = rope(u_ref[:, MLA_Q_RANK + MLA_KV_RANK:]).astype(k_ref.dtype)
    for h in range(N_HEADS):
        base = h * MLA_QK
        q_ref[:, base:base + MLA_NOPE] = q[:, base:base + MLA_NOPE].astype(q_ref.dtype)
        q_ref[:, base + MLA_NOPE:base + MLA_QK] = rope(q[:, base + MLA_NOPE:base + MLA_QK]).astype(q_ref.dtype)
        k_ref[:, base:base + MLA_NOPE] = kv[:, h * MLA_NOPE:(h + 1) * MLA_NOPE].astype(k_ref.dtype)
        k_ref[:, base + MLA_NOPE:base + MLA_QK] = k_rope
    v_ref[...] = kv[:, N_HEADS * MLA_NOPE:].astype(v_ref.dtype)


def _mla_prep(u_mla, g_q, g_kv, w_uq, w_ukv, cos_m, sin_m, tm=512):
    T = u_mla.shape[0]
    row = lambda w: pl.BlockSpec((tm, w), lambda i: (i, 0))
    full = lambda a: pl.BlockSpec(a.shape, lambda i: (0, 0))
    return pl.pallas_call(
        _mla_prep_body,
        grid=(T // tm,),
        in_specs=[row(MLA_W), full(g_q), full(g_kv), full(w_uq), full(w_ukv), row(LANES), row(LANES)],
        out_specs=[row(N_HEADS * MLA_QK), row(N_HEADS * MLA_QK), row(GROUP_W)],
        out_shape=[jax.ShapeDtypeStruct((T, N_HEADS * MLA_QK), jnp.bfloat16),
                   jax.ShapeDtypeStruct((T, N_HEADS * MLA_QK), jnp.bfloat16),
                   jax.ShapeDtypeStruct((T, GROUP_W), jnp.bfloat16)],
        compiler_params=_cparams(("arbitrary",)),
        name="mla_prep",
    )(u_mla, g_q, g_kv, w_uq, w_ukv, cos_m, sin_m)


def _mla_body(q_ref, k_ref, v_ref, o_ref):
    j = pl.program_id(1)
    qk_cols = [slice(h * MLA_QK, (h + 1) * MLA_QK) for h in range(N_HEADS)]
    v_cols = [slice(h * HEAD_DIM, (h + 1) * HEAD_DIM) for h in range(N_HEADS)]
    qs = [q_ref[:, c] for c in qk_cols]
    done = _sweep_key_tiles(j, N_HEADS,
                            lambda h, kb, ks: _nt_dot(qs[h], k_ref[pl.ds(ks, ATT_BLK), qk_cols[h]]),
                            lambda h, ks: v_ref[pl.ds(ks, ATT_BLK), v_cols[h]])
    for h, (_, l, acc) in enumerate(done):
        o_ref[:, v_cols[h]] = (acc / l).astype(o_ref.dtype)


def _mla_attention(q, k, v, B, S):
    nq = S // ATT_BLK
    return pl.pallas_call(
        _mla_body,
        grid=(B, nq),
        in_specs=[pl.BlockSpec((ATT_BLK, N_HEADS * MLA_QK), lambda b, j: (b * nq + j, 0)),
                  pl.BlockSpec((S, N_HEADS * MLA_QK), lambda b, j: (b, 0)),
                  pl.BlockSpec((S, GROUP_W), lambda b, j: (b, 0))],
        out_specs=pl.BlockSpec((ATT_BLK, GROUP_W), lambda b, j: (b * nq + j, 0)),
        out_shape=jax.ShapeDtypeStruct((B * S, GROUP_W), jnp.bfloat16),
        compiler_params=_cparams(("arbitrary", "arbitrary")),
        name="mla_attention",
    )(q, k, v)


def _layer_norm_rows(h, g, b):
    mu = jnp.mean(h, axis=1, keepdims=True)
    d = h - mu
    var = jnp.mean(jnp.square(d), axis=1, keepdims=True)
    return d * lax.rsqrt(var + EPS) * g + b


def _pack_bf16_pairs(x):
    n = x.shape[1] // 2
    lo = lax.bitcast_convert_type(x[:, :n].astype(jnp.bfloat16).astype(jnp.float32), jnp.int32)
    hi = lax.bitcast_convert_type(x[:, n:].astype(jnp.bfloat16).astype(jnp.float32), jnp.int32)
    return lax.shift_right_logical(lo, 16) | (hi & jnp.int32(-65536))


def _unpack_bf16_pairs(u):
    lo = lax.bitcast_convert_type(lax.shift_left(u, 16), jnp.float32).astype(jnp.bfloat16)
    hi = lax.bitcast_convert_type(u & jnp.int32(-65536), jnp.float32).astype(jnp.bfloat16)
    return lo, hi


def _wo_ln_body(ya_ref, yb_ref, yc_ref, yd_ref, wo_ref, x_ref, g_ref, b_ref, o_ref, op_ref, *, alpha):
    h = alpha * x_ref[...]
    for gi, y_ref in enumerate((ya_ref, yb_ref, yc_ref, yd_ref)):
        h = h + jnp.dot(y_ref[...], wo_ref[gi * GROUP_W:(gi + 1) * GROUP_W, :], preferred_element_type=jnp.float32)
    out = _layer_norm_rows(h, g_ref[...], b_ref[...])
    o_ref[...] = out
    op_ref[...] = _pack_bf16_pairs(out)


def _wo_ln(ys, w_o, x, g, b, alpha, tm=256):
    T = x.shape[0]
    row = lambda w: pl.BlockSpec((tm, w), lambda i: (i, 0))
    full = lambda a: pl.BlockSpec(a.shape, lambda i: (0, 0))
    return pl.pallas_call(
        functools.partial(_wo_ln_body, alpha=alpha),
        grid=(T // tm,),
        in_specs=[row(GROUP_W)] * 4 + [full(w_o), row(D_MODEL), full(g), full(b)],
        out_specs=[row(D_MODEL), row(D_MODEL // 2)],
        out_shape=[jax.ShapeDtypeStruct((T, D_MODEL), jnp.float32),
                   jax.ShapeDtypeStruct((T, D_MODEL // 2), jnp.int32)],
        compiler_params=_cparams(("arbitrary",)),
        name="wo_layernorm",
    )(*ys, w_o, x, g, b)


def _lane_slab(cols, dtype):
    rows = cols[0].shape[0]
    lane = lax.broadcasted_iota(jnp.int32, (rows, LANES), 1)
    out = jnp.zeros((rows, LANES), dtype)
    for i, c in enumerate(cols):
        out = jnp.where(lane == i, c.astype(dtype), out)
    return out


def _router_body(x_ref, w_ref, b_ref, idx_ref, gate_ref, rank_ref, cnt_ref, carry_scr):
    tm = x_ref.shape[0]

    @pl.when(pl.program_id(0) == 0)
    def _():
        carry_scr[...] = jnp.zeros_like(carry_scr)

    logits = jnp.dot(x_ref[...], w_ref[...], preferred_element_type=jnp.float32,
                     precision=lax.Precision.HIGHEST) + b_ref[...]
    lane = lax.broadcasted_iota(jnp.int32, logits.shape, 1)
    vals, idxs, hots = [], [], []
    for _ in range(TOP_K):
        m = jnp.max(logits, axis=1, keepdims=True)
        idx = jnp.min(jnp.where(logits == m, lane, N_EXPERTS), axis=1, keepdims=True)
        hot = lane == idx
        vals.append(m)
        idxs.append(idx)
        hots.append(hot)
        logits = jnp.where(hot, -jnp.inf, logits)
    exps = [jnp.exp(v - vals[0]) for v in vals]
    tot = exps[0] + exps[1] + exps[2] + exps[3]
    gates = [e / tot for e in exps]

    hot_f = [h.astype(jnp.float32) for h in hots]
    hot_all = hot_f[0] + hot_f[1] + hot_f[2] + hot_f[3]
    r = lax.broadcasted_iota(jnp.int32, (tm, tm), 0)
    c = lax.broadcasted_iota(jnp.int32, (tm, tm), 1)
    before = (c < r).astype(jnp.bfloat16)
    prior = jnp.dot(before, hot_all.astype(jnp.bfloat16), preferred_element_type=jnp.float32) + carry_scr[...]
    ranks = [jnp.sum(h * prior, axis=1, keepdims=True) for h in hot_f]
    carry_scr[...] = carry_scr[...] + jnp.sum(hot_all, axis=0, keepdims=True)

    idx_ref[...] = _lane_slab(idxs, jnp.int32)
    gate_ref[...] = _lane_slab(gates, jnp.float32)
    rank_ref[...] = _lane_slab(ranks, jnp.int32)
    cnt_ref[...] = carry_scr[...]


def _router(x, w_router, b_router, tm=512):
    T = x.shape[0]
    row = lambda w: pl.BlockSpec((tm, w), lambda i: (i, 0))
    full = lambda a: pl.BlockSpec(a.shape, lambda i: (0, 0))
    slab = lambda dt: jax.ShapeDtypeStruct((T, LANES), dt)
    return pl.pallas_call(
        _router_body,
        grid=(T // tm,),
        in_specs=[row(D_MODEL), full(w_router), full(b_router)],
        out_specs=[row(LANES), row(LANES), row(LANES), pl.BlockSpec((1, N_EXPERTS), lambda i: (0, 0))],
        out_shape=[slab(jnp.int32), slab(jnp.float32), slab(jnp.int32),
                   jax.ShapeDtypeStruct((1, N_EXPERTS), jnp.float32)],
        scratch_shapes=[pltpu.VMEM((1, N_EXPERTS), jnp.float32)],
        compiler_params=_cparams(("arbitrary",)),
        name="router",
    )(x, w_router, b_router)


def _dispatch_body(dest_ref, x_ref, zero_hbm, o_hbm, sem, *, tm):
    del zero_hbm

    def start(n, _):
        pltpu.make_async_copy(x_ref.at[pl.ds(n // TOP_K, 1)], o_hbm.at[pl.ds(dest_ref[n], 1)], sem).start()
        return 0

    lax.fori_loop(0, tm * TOP_K, start, 0, unroll=8)
    for _ in range(TOP_K):
        pltpu.make_async_copy(x_ref, o_hbm.at[pl.ds(0, tm)], sem).wait()


def _dispatch(dest_flat, x_packed, n_rows, tm=512):
    T, W = x_packed.shape
    zeros = jnp.zeros((n_rows, W), x_packed.dtype)
    return pl.pallas_call(
        functools.partial(_dispatch_body, tm=tm),
        grid=(T // tm,),
        in_specs=[pl.BlockSpec((tm * TOP_K,), lambda i: (i,), memory_space=pltpu.SMEM),
                  pl.BlockSpec((tm, W), lambda i: (i, 0)), pl.BlockSpec(memory_space=pl.ANY)],
        out_specs=pl.BlockSpec(memory_space=pl.ANY),
        out_shape=jax.ShapeDtypeStruct((n_rows, W), x_packed.dtype),
        scratch_shapes=[pltpu.SemaphoreType.DMA(())],
        input_output_aliases={2: 0},
        compiler_params=_cparams(("arbitrary",)),
        name="moe_dispatch",
    )(dest_flat, x_packed, zeros)


def _ffn_body(be_ref, nu_ref, x_ref, wgu_ref, bgu_ref, wdn_ref, bdn_ref, y_ref):
    del be_ref

    @pl.when(pl.program_id(0) < nu_ref[0])
    def _():
        lo, hi = _unpack_bf16_pairs(x_ref[...])
        half = D_MODEL // 2
        h = (jnp.dot(lo, wgu_ref[0, :half, :], preferred_element_type=jnp.float32)
             + jnp.dot(hi, wgu_ref[0, half:, :], preferred_element_type=jnp.float32) + bgu_ref[0])
        glu = jnp.minimum(h[:, :D_EXPERT], SWIGLU_LIMIT)
        lin = jnp.clip(h[:, D_EXPERT:], -SWIGLU_LIMIT, SWIGLU_LIMIT)
        a = glu * jax.nn.sigmoid(SWIGLU_ALPHA * glu) * (lin + 1.0)
        y_ref[...] = jnp.dot(a.astype(jnp.bfloat16), wdn_ref[0], preferred_element_type=jnp.float32) + bdn_ref[0]

    @pl.when(pl.program_id(0) >= nu_ref[0])
    def _():
        y_ref[...] = jnp.zeros_like(y_ref)


def _expert_ffn(block_e, n_used, xg, w_gu, b_gu, w_dn, b_dn):
    P = xg.shape[0]
    n_blocks = P // MOE_BLK
    per_e = lambda a: pl.BlockSpec((1,) + a.shape[1:], lambda i, be, nu: (be[i], 0, 0))
    grid_spec = pltpu.PrefetchScalarGridSpec(
        num_scalar_prefetch=2,
        grid=(n_blocks,),
        in_specs=[pl.BlockSpec((MOE_BLK, D_MODEL // 2), lambda i, be, nu: (i, 0)),
                  per_e(w_gu), per_e(b_gu), per_e(w_dn), per_e(b_dn)],
        out_specs=pl.BlockSpec((MOE_BLK, D_MODEL), lambda i, be, nu: (i, 0)),
    )
    return pl.pallas_call(
        _ffn_body,
        grid_spec=grid_spec,
        out_shape=jax.ShapeDtypeStruct((P, D_MODEL), jnp.float32),
        compiler_params=_cparams(("arbitrary",)),
        name="expert_ffn",
    )(block_e, n_used, xg, w_gu, b_gu, w_dn, b_dn)


def _combine_body(dest_ref, dest_next_ref, y_hbm, x_ref, gate_ref, g_ref, b_ref, o_ref, ob_ref, buf, sem, *, alpha, tm):
    i = pl.program_id(0)
    slot = i % 2

    def gather_rows(idx_ref, s):
        def start(n, _):
            pltpu.make_async_copy(y_hbm.at[pl.ds(idx_ref[n], 1)], buf.at[s, n % TOP_K, pl.ds(n // TOP_K, 1)],
                                  sem.at[s]).start()
            return 0
        lax.fori_loop(0, tm * TOP_K, start, 0, unroll=8)

    @pl.when(i == 0)
    def _():
        gather_rows(dest_ref, 0)

    @pl.when(i + 1 < pl.num_programs(0))
    def _():
        gather_rows(dest_next_ref, 1 - slot)

    for jj in range(TOP_K):
        pltpu.make_async_copy(y_hbm.at[pl.ds(0, tm)], buf.at[slot, jj], sem.at[slot]).wait()

    h = alpha * x_ref[...]
    gates = gate_ref[...]
    for jj in range(TOP_K):
        h = h + gates[:, jj:jj + 1] * buf[slot, jj]
    out = _layer_norm_rows(h, g_ref[...], b_ref[...])
    o_ref[...] = out
    ob_ref[...] = out.astype(ob_ref.dtype)


def _combine(dest_flat, y, x, gates, g, b, alpha, tm=128):
    T = x.shape[0]
    n_tiles = T // tm
    row = lambda w: pl.BlockSpec((tm, w), lambda i: (i, 0))
    full = lambda a: pl.BlockSpec(a.shape, lambda i: (0, 0))
    return pl.pallas_call(
        functools.partial(_combine_body, alpha=alpha, tm=tm),
        grid=(n_tiles,),
        in_specs=[pl.BlockSpec((tm * TOP_K,), lambda i: (i,), memory_space=pltpu.SMEM),
                  pl.BlockSpec((tm * TOP_K,), lambda i: (jnp.minimum(i + 1, n_tiles - 1),), memory_space=pltpu.SMEM),
                  pl.BlockSpec(memory_space=pl.ANY), row(D_MODEL), row(LANES), full(g), full(b)],
        out_specs=[row(D_MODEL), row(D_MODEL)],
        out_shape=[jax.ShapeDtypeStruct((T, D_MODEL), jnp.float32),
                   jax.ShapeDtypeStruct((T, D_MODEL), jnp.bfloat16)],
        scratch_shapes=[pltpu.VMEM((2, TOP_K, tm, D_MODEL), jnp.float32), pltpu.SemaphoreType.DMA((2,))],
        compiler_params=_cparams(("arbitrary",)),
        name="moe_combine",
    )(dest_flat, dest_flat, y, x, gates, g, b)


def _t5_bucket(dist):
    n = jnp.maximum(dist, 0)
    exact = T5_BUCKETS // 2
    nf = jnp.maximum(n, 1).astype(jnp.float32)
    large = exact + (jnp.log(nf / exact) / math.log(T5_MAX_DIST / exact) * (T5_BUCKETS - exact)).astype(jnp.int32)
    return jnp.where(n < exact, n, jnp.minimum(large, T5_BUCKETS - 1))


def _toeplitz_bias(rel, S):
    n_heads = rel.shape[1]
    period = S + ATT_BLK
    k = jnp.arange(period)
    dist = (S - ATT_BLK) - jnp.where(k < S, k, k - period)
    by_dist = jnp.transpose(rel[_t5_bucket(jnp.arange(S))].astype(jnp.float32))
    row0 = jnp.where((dist >= 0) & (dist < S), by_dist[:, jnp.clip(dist, 0, S - 1)], 0.0)
    flat = jnp.tile(row0, (1, ATT_BLK + 1))[:, :ATT_BLK * (period - 1)]
    return flat.reshape(n_heads, ATT_BLK, period - 1)[:, :, :S]


def _band_bias(rel):
    m = BAND + jnp.arange(BAND)[:, None] - jnp.arange(2 * BAND)[None, :]
    out = []
    for window, dil in DIL_PATTERNS:
        keep = (m >= 0) & (m <= window // dil)
        tab = jnp.transpose(rel[_t5_bucket(m * dil)].astype(jnp.float32), (2, 0, 1))
        out.append(jnp.where(keep[None], tab, NEG))
    return jnp.stack(out, 0)


def _layer_weights(w_in_l, w_uq_l, w_ukv_l):
    bf = jnp.bfloat16
    o = np.cumsum((0, 512, 512, 512, 1024, 64, 16, 512, 512, 512, 512, 512, 512, 384, 256, 64))
    seg = lambda a, b: w_in_l[:, int(o[a]):int(o[b])]
    sa, sb, sd = HEAD_DIM ** -0.5, DIFF_QK ** -0.5, (MLA_NOPE + MLA_ROPE) ** -0.5
    w_ab = jnp.concatenate([seg(0, 1) * sa, seg(1, 3), seg(6, 7) * sb, seg(7, 9)], axis=1).astype(bf)
    w_c = jnp.concatenate([seg(9, 10) * sa, seg(10, 12)], axis=1).astype(bf)
    w_uq_l = w_uq_l * sd
    pad = lambda n: jnp.zeros((D_MODEL, n), w_in_l.dtype)
    w_idx = jnp.concatenate([seg(3, 6), pad(IDX_W - 1104)], axis=1).astype(bf)
    w_mla = jnp.concatenate([seg(12, 15), pad(MLA_W - 704)], axis=1).astype(bf)
    uq = w_uq_l.reshape(MLA_Q_RANK, N_HEADS, MLA_NOPE + MLA_ROPE)
    uq = jnp.pad(uq, ((0, 0), (0, 0), (0, MLA_QK - MLA_NOPE - MLA_ROPE))).reshape(MLA_Q_RANK, N_HEADS * MLA_QK)
    ukv = w_ukv_l.reshape(MLA_KV_RANK, N_HEADS, 2, MLA_NOPE)
    ukv = jnp.transpose(ukv, (0, 2, 1, 3)).reshape(MLA_KV_RANK, 2 * N_HEADS * MLA_NOPE)
    return w_ab, w_c, w_idx, w_mla, uq.astype(bf), ukv.astype(bf)


def kernel(x, positions, rel_bias, w_in, w_o, ln1_g, ln1_b, ln2_g, ln2_b, mla_g_q, mla_w_uq, mla_g_kv, mla_w_ukv,
           diff_lq1, diff_lk1, diff_lq2, diff_lk2, diff_subln_g, w_router, b_router, w_gate_up, b_gate_up, w_down,
           b_down):
    B, S, D = x.shape
    depth = w_in.shape[0]
    T = B * S
    assert D == D_MODEL and S % (16 * BAND) == 0 and T % 1024 == 0
    alpha = (2 * depth) ** 0.25
    f32 = jnp.float32

    bias_a = _toeplitz_bias(rel_bias[:, :N_HEADS], S)
    bias_b = _toeplitz_bias(rel_bias[:, N_HEADS:2 * N_HEADS], S)
    bias_c = _band_bias(rel_bias[:, 2 * N_HEADS:])
    cos_i, sin_i = _rope_tables(positions, IDX_ROPE // 2, IDX_DIM)
    cos_m, sin_m = _rope_tables(positions, MLA_ROPE // 2, LANES)

    n_blocks = T * TOP_K // MOE_BLK + N_EXPERTS
    n_rows = n_blocks * MOE_BLK

    xf = x.reshape(T, D)
    xb = xf.astype(jnp.bfloat16)
    for l in range(depth):
        w_ab, w_c, w_idx, w_mla, w_uq, w_ukv = _layer_weights(w_in[l], mla_w_uq[l], mla_w_ukv[l])
        u_ab = _matmul(xb, w_ab, jnp.bfloat16, 1024, 512)
        u_c = _matmul(xb, w_c, f32, 1024, 512)
        u_idx = _matmul(xb, w_idx, f32, 1024, IDX_W // 3)
        u_mla = _matmul(xb, w_mla, f32, 1024, MLA_W)

        qi, kie, kio, w_i = _idx_prep(u_idx, cos_i, sin_i)
        ya = _dsa_attention(qi, kie, kio, w_i, u_ab, bias_a, B, S)

        lam_init = 0.8 - 0.6 * math.exp(-0.3 * l)
        lam = (jnp.exp(jnp.sum(diff_lq1[l].astype(f32) * diff_lk1[l].astype(f32)))
               - jnp.exp(jnp.sum(diff_lq2[l].astype(f32) * diff_lk2[l].astype(f32))) + lam_init).reshape(1)
        yb = _diff_attention(lam, u_ab, bias_b, diff_subln_g[l].reshape(1, HEAD_DIM).astype(f32), lam_init, B, S)

        yc = _dil_attention(u_c, bias_c, B, S)

        qd, kd, vd = _mla_prep(u_mla, mla_g_q[l].reshape(1, -1).astype(f32), mla_g_kv[l].reshape(1, -1).astype(f32),
                               w_uq, w_ukv, cos_m, sin_m)
        yd = _mla_attention(qd, kd, vd, B, S)

        x1, x1p = _wo_ln((ya, yb, yc, yd), w_o[l].astype(jnp.bfloat16), xf,
                         ln1_g[l].reshape(1, D).astype(f32), ln1_b[l].reshape(1, D).astype(f32), alpha)

        idx_s, gate_s, rank_s, counts = _router(x1, w_router[l].astype(f32), b_router[l].reshape(1, -1).astype(f32))
        counts = counts.reshape(N_EXPERTS).astype(jnp.int32)
        padded = (counts + MOE_BLK - 1) // MOE_BLK * MOE_BLK
        pad_end = jnp.cumsum(padded)
        pad_start = pad_end - padded
        dest = (pad_start[idx_s[:, :TOP_K]] + rank_s[:, :TOP_K]).reshape(T * TOP_K).astype(jnp.int32)
        block_e = jnp.minimum(jnp.searchsorted(pad_end, jnp.arange(n_blocks) * MOE_BLK, side='right'),
                              N_EXPERTS - 1).astype(jnp.int32)
        n_used = (pad_end[-1:] // MOE_BLK).astype(jnp.int32)

        xg = _dispatch(dest, x1p, n_rows)
        y = _expert_ffn(block_e, n_used, xg, w_gate_up[l].astype(jnp.bfloat16),
                        b_gate_up[l].reshape(N_EXPERTS, 1, -1).astype(f32), w_down[l].astype(jnp.bfloat16),
                        b_down[l].reshape(N_EXPERTS, 1, -1).astype(f32))
        xf, xb = _combine(dest, y, x1, gate_s, ln2_g[l].reshape(1, D).astype(f32),
                          ln2_b[l].reshape(1, D).astype(f32), alpha)
    return xf.reshape(B, S, D)
```

```python
import functools
import math

import jax
import jax.numpy as jnp
import numpy as np
from jax import lax
from jax.experimental import pallas as pl
from jax.experimental.pallas import tpu as pltpu

D_MODEL = 2048
HEAD_DIM = 128
N_HEADS = 4
GROUP_W = N_HEADS * HEAD_DIM
IDX_HEADS = 16
IDX_DIM = 64
IDX_ROPE = 32
DSA_TOPK = 256
DIFF_QK = 64
DIL_PATTERNS = ((128, 1), (512, 4), (2048, 16))
BAND = 128
MLA_Q_RANK = 384
MLA_KV_RANK = 256
MLA_NOPE = 128
MLA_ROPE = 64
T5_BUCKETS = 32
T5_MAX_DIST = 2048
ROPE_THETA = 10000.0
N_EXPERTS = 32
TOP_K = 4
D_EXPERT = 768
SWIGLU_ALPHA = 1.702
SWIGLU_LIMIT = 7.0
EPS = 1e-5

LANES = 128
ATT_BLK = 256
MOE_BLK = 512
NEG = -1e30
INT_MIN = -2 ** 31
VMEM_LIMIT = 56 * 1024 * 1024

IDX_W = IDX_HEADS * IDX_DIM + LANES
MLA_W = 768
MLA_QK = 256


def _cparams(sem):
    return pltpu.CompilerParams(dimension_semantics=sem, vmem_limit_bytes=VMEM_LIMIT)


def _matmul_body(x_ref, w_ref, o_ref):
    o_ref[...] = jnp.dot(x_ref[...], w_ref[...], preferred_element_type=jnp.float32).astype(o_ref.dtype)


def _matmul(x, w, out_dtype, tm, tn):
    M, K = x.shape
    N = w.shape[1]
    return pl.pallas_call(
        _matmul_body,
        grid=(N // tn, M // tm),
        in_specs=[pl.BlockSpec((tm, K), lambda n, m: (m, 0)),
                  pl.BlockSpec((K, tn), lambda n, m: (0, n))],
        out_specs=pl.BlockSpec((tm, tn), lambda n, m: (m, n)),
        out_shape=jax.ShapeDtypeStruct((M, N), out_dtype),
        compiler_params=_cparams(("arbitrary", "arbitrary")),
        name="proj_matmul",
    )(x, w)


def _rot_partner(x, half):
    width = x.shape[-1]
    lane = lax.broadcasted_iota(jnp.int32, x.shape, x.ndim - 1)
    fwd = pltpu.roll(x, width - half, x.ndim - 1)
    bwd = pltpu.roll(x, half, x.ndim - 1)
    return jnp.where((lane % (2 * half)) < half, fwd, bwd)


def _rope_tables(positions, half, period):
    pos = positions.reshape(-1).astype(jnp.float32)
    inv = ROPE_THETA ** (-jnp.arange(half, dtype=jnp.float32) / half)
    ang = pos[:, None] * inv
    cos, sin = jnp.cos(ang), jnp.sin(ang)
    rest = period - 2 * half
    ones = jnp.ones((pos.shape[0], rest), jnp.float32)
    zeros = jnp.zeros((pos.shape[0], rest), jnp.float32)
    c = jnp.concatenate([cos, cos, ones], axis=1)
    s = jnp.concatenate([-sin, sin, zeros], axis=1)
    reps = LANES // period
    return jnp.tile(c, (1, reps)), jnp.tile(s, (1, reps))


def _idx_prep_body(u_ref, cos_ref, sin_ref, qi_ref, kie_ref, kio_ref, w_ref):
    cos = cos_ref[...]
    sin = sin_ref[...]
    half = IDX_ROPE // 2
    for c in range(IDX_HEADS * IDX_DIM // LANES):
        x = u_ref[:, c * LANES:(c + 1) * LANES]
        qi_ref[:, c * LANES:(c + 1) * LANES] = (x * cos + _rot_partner(x, half) * sin).astype(qi_ref.dtype)
    tail = u_ref[:, IDX_HEADS * IDX_DIM:]
    lane = lax.broadcasted_iota(jnp.int32, tail.shape, 1)
    roped = tail * cos + _rot_partner(tail, half) * sin
    even = jnp.where(lane < IDX_DIM, roped, 0.0)
    kie_ref[...] = even.astype(kie_ref.dtype)
    kio_ref[...] = pltpu.roll(even, IDX_DIM, 1).astype(kio_ref.dtype)
    w_ref[...] = tail * (IDX_HEADS ** -0.5 * IDX_DIM ** -0.5)


def _idx_prep(u_idx, cos_i, sin_i, tm=512):
    T = u_idx.shape[0]
    row = lambda w: pl.BlockSpec((tm, w), lambda i: (i, 0))
    return pl.pallas_call(
        _idx_prep_body,
        grid=(T // tm,),
        in_specs=[row(IDX_W), row(LANES), row(LANES)],
        out_specs=[row(IDX_HEADS * IDX_DIM), row(LANES), row(LANES), row(LANES)],
        out_shape=[jax.ShapeDtypeStruct((T, IDX_HEADS * IDX_DIM), jnp.bfloat16),
                   jax.ShapeDtypeStruct((T, LANES), jnp.bfloat16),
                   jax.ShapeDtypeStruct((T, LANES), jnp.bfloat16),
                   jax.ShapeDtypeStruct((T, LANES), jnp.float32)],
        compiler_params=_cparams(("arbitrary",)),
        name="idx_prep",
    )(u_idx, cos_i, sin_i)


def _nt_dot(a, b):
    return lax.dot_general(a, b, (((1,), (1,)), ((), ())), preferred_element_type=jnp.float32)


def _online_step(s, v_tile, carry):
    m, l, acc = carry
    m_new = jnp.maximum(m, jnp.max(s, axis=1, keepdims=True))
    alpha = jnp.exp(m - m_new)
    p = jnp.exp(s - m_new)
    l_new = alpha * l + jnp.sum(p, axis=1, keepdims=True)
    acc_new = alpha * acc + jnp.dot(p.astype(v_tile.dtype), v_tile, preferred_element_type=jnp.float32)
    return m_new, l_new, acc_new


def _softmax_init(rows, dv):
    return (jnp.full((rows, 1), NEG, jnp.float32), jnp.zeros((rows, 1), jnp.float32),
            jnp.zeros((rows, dv), jnp.float32))


def _causal_keep(j, kb):
    row = lax.broadcasted_iota(jnp.int32, (ATT_BLK, ATT_BLK), 0)
    col = lax.broadcasted_iota(jnp.int32, (ATT_BLK, ATT_BLK), 1)
    return (col - row) <= (j - kb) * ATT_BLK


def _sweep_key_tiles(j, n_chains, logits_fn, value_fn):
    def step(kb, carries, diagonal):
        ks = pl.multiple_of(kb * ATT_BLK, ATT_BLK)
        out = []
        for c in range(n_chains):
            s = logits_fn(c, kb, ks)
            if diagonal:
                s = jnp.where(_causal_keep(0, 0), s, NEG)
            out.append(_online_step(s, value_fn(c, ks), carries[c]))
        return tuple(out)

    init = tuple(_softmax_init(ATT_BLK, HEAD_DIM) for _ in range(n_chains))
    carries = lax.fori_loop(0, j, functools.partial(step, diagonal=False), init)
    return step(j, carries, True)


def _float_key(score):
    bits = lax.bitcast_convert_type(score + 0.0, jnp.int32)
    return bits ^ (lax.shift_right_arithmetic(bits, 31) & jnp.int32(0x7FFFFFFF))


def _dsa_body(qi_ref, kie_ref, kio_ref, w_ref, q_ref, k_ref, v_ref, bias_ref, o_ref, key_scr, mask_scr, *, n_sel):
    j = pl.program_id(1)
    n_tiles = j + 1
    w = w_ref[...]

    def score_tile(kb, _):
        ks = pl.multiple_of(kb * ATT_BLK, ATT_BLK)
        kie = kie_ref[pl.ds(ks, ATT_BLK), :]
        kio = kio_ref[pl.ds(ks, ATT_BLK), :]
        score = jnp.zeros((ATT_BLK, ATT_BLK), jnp.float32)
        for h in range(IDX_HEADS):
            pair = qi_ref[:, (h // 2) * LANES:(h // 2 + 1) * LANES]
            dots = _nt_dot(pair, kie if h % 2 == 0 else kio)
            score = score + jnp.maximum(dots, 0.0) * w[:, IDX_DIM + h:IDX_DIM + h + 1]
        key = jnp.where(_causal_keep(j, kb), _float_key(score), jnp.int32(INT_MIN))
        key_scr[:, pl.ds(ks, ATT_BLK)] = key
        return 0

    lax.fori_loop(0, n_tiles, score_tile, 0)

    def count_ge(cand):
        def body(kb, cnt):
            ks = pl.multiple_of(kb * ATT_BLK, ATT_BLK)
            hit = jnp.where(key_scr[:, pl.ds(ks, ATT_BLK)] >= cand, 1.0, 0.0)
            return cnt + hit[:, :LANES] + hit[:, LANES:]
        cnt = lax.fori_loop(0, n_tiles, body, jnp.zeros((ATT_BLK, LANES), jnp.float32))
        return jnp.sum(cnt, axis=1, keepdims=True)

    zero = jnp.zeros((ATT_BLK, 1), jnp.int32)
    thr = jnp.where(count_ge(zero) >= n_sel, zero, jnp.int32(INT_MIN))

    def bit_step(i, thr):
        cand = thr | lax.shift_left(jnp.int32(1), jnp.int32(30) - i)
        return jnp.where(count_ge(cand) >= n_sel, cand, thr)

    thr = lax.fori_loop(0, 31, bit_step, thr)

    def mask_tile(kb, _):
        ks = pl.multiple_of(kb * ATT_BLK, ATT_BLK)
        key = key_scr[:, pl.ds(ks, ATT_BLK)]
        sel = (key >= thr) & (key != jnp.int32(INT_MIN))
        mask_scr[:, pl.ds(ks, ATT_BLK)] = jnp.where(sel, 0.0, NEG)
        return 0

    lax.fori_loop(0, n_tiles, mask_tile, 0)

    n_bias_tiles = bias_ref.shape[2] // ATT_BLK
    cols = [slice(h * HEAD_DIM, (h + 1) * HEAD_DIM) for h in range(N_HEADS)]
    qs = [q_ref[:, c] for c in cols]

    def logits(h, kb, ks):
        bs = pl.multiple_of((kb + n_bias_tiles - 1 - j) * ATT_BLK, ATT_BLK)
        s = _nt_dot(qs[h], k_ref[pl.ds(ks, ATT_BLK), cols[h]])
        return s + bias_ref[h, :, pl.ds(bs, ATT_BLK)] + mask_scr[:, pl.ds(ks, ATT_BLK)]

    done = _sweep_key_tiles(j, N_HEADS, logits, lambda h, ks: v_ref[pl.ds(ks, ATT_BLK), cols[h]])
    for h, (_, l, acc) in enumerate(done):
        o_ref[:, cols[h]] = (acc / l).astype(o_ref.dtype)


def _dsa_attention(qi, kie, kio, w_idx, qkv, bias_u, B, S):
    nq = S // ATT_BLK
    n_sel = min(DSA_TOPK, S // 4)
    qrow = lambda w, c: pl.BlockSpec((ATT_BLK, w), lambda b, j: (b * nq + j, c))
    seq = lambda w, c: pl.BlockSpec((S, w), lambda b, j: (b, c))
    return pl.pallas_call(
        functools.partial(_dsa_body, n_sel=n_sel),
        grid=(B, nq),
        in_specs=[qrow(IDX_HEADS * IDX_DIM, 0), seq(LANES, 0), seq(LANES, 0), qrow(LANES, 0),
                  qrow(GROUP_W, 0), seq(GROUP_W, 1), seq(GROUP_W, 2),
                  pl.BlockSpec(bias_u.shape, lambda b, j: (0, 0, 0))],
        out_specs=qrow(GROUP_W, 0),
        out_shape=jax.ShapeDtypeStruct((B * S, GROUP_W), jnp.bfloat16),
        scratch_shapes=[pltpu.VMEM((ATT_BLK, S), jnp.int32), pltpu.VMEM((ATT_BLK, S), jnp.float32)],
        compiler_params=_cparams(("arbitrary", "arbitrary")),
        name="dsa_attention",
    )(qi, kie, kio, w_idx, qkv, qkv, qkv, bias_u)


def _diff_body(lam_ref, q_ref, k_ref, v_ref, bias_ref, g_ref, o_ref, *, out_scale):
    j = pl.program_id(1)
    lam = lam_ref[0]
    n_bias_tiles = bias_ref.shape[2] // ATT_BLK
    lane = lax.broadcasted_iota(jnp.int32, (ATT_BLK, HEAD_DIM), 1)
    cols = [slice(h * HEAD_DIM, (h + 1) * HEAD_DIM) for h in range(N_HEADS)]
    qs = []
    for h in range(N_HEADS):
        q = q_ref[:, cols[h]]
        qs += [jnp.where(lane < DIFF_QK, q, jnp.zeros_like(q)), jnp.where(lane >= DIFF_QK, q, jnp.zeros_like(q))]

    def logits(c, kb, ks):
        bs = pl.multiple_of((kb + n_bias_tiles - 1 - j) * ATT_BLK, ATT_BLK)
        return _nt_dot(qs[c], k_ref[pl.ds(ks, ATT_BLK), cols[c // 2]]) + bias_ref[c // 2, :, pl.ds(bs, ATT_BLK)]

    done = _sweep_key_tiles(j, 2 * N_HEADS, logits, lambda c, ks: v_ref[pl.ds(ks, ATT_BLK), cols[c // 2]])
    for h in range(N_HEADS):
        (_, l1, a1), (_, l2, a2) = done[2 * h], done[2 * h + 1]
        y = a1 / l1 - lam * (a2 / l2)
        y = y * lax.rsqrt(jnp.mean(jnp.square(y), axis=1, keepdims=True) + EPS) * g_ref[...]
        o_ref[:, cols[h]] = (y * out_scale).astype(o_ref.dtype)


def _diff_attention(lam, qkv, bias_u, subln_g, lam_init, B, S):
    nq = S // ATT_BLK
    qrow = lambda c: pl.BlockSpec((ATT_BLK, GROUP_W), lambda b, j: (b * nq + j, c))
    seq = lambda c: pl.BlockSpec((S, GROUP_W), lambda b, j: (b, c))
    return pl.pallas_call(
        functools.partial(_diff_body, out_scale=1.0 - lam_init),
        grid=(B, nq),
        in_specs=[pl.BlockSpec(memory_space=pltpu.SMEM),
                  qrow(3), seq(4), seq(5),
                  pl.BlockSpec(bias_u.shape, lambda b, j: (0, 0, 0)),
                  pl.BlockSpec((1, HEAD_DIM), lambda b, j: (0, 0))],
        out_specs=qrow(0),
        out_shape=jax.ShapeDtypeStruct((B * S, GROUP_W), jnp.bfloat16),
        compiler_params=_cparams(("arbitrary", "arbitrary")),
        name="diff_attention",
    )(lam, qkv, qkv, qkv, bias_u, subln_g)


def _dil_body(q_ref, k_ref, v_ref, bias_ref, o_ref, out_scr, lse_scr):
    S = q_ref.shape[0]
    col = lax.broadcasted_iota(jnp.int32, (BAND, 2 * BAND), 1)
    for p, (_, dil) in enumerate(DIL_PATTERNS):
        n_blocks = S // dil // BAND
        bias = bias_ref[p, 0]

        def band_block(it, _, p=p, dil=dil, bias=bias):
            r = it % dil
            blk = it // dil
            cur = r + dil * BAND * blk
            prev = r + dil * BAND * jnp.maximum(blk - 1, 0)
            rows = lambda ref, start: ref[pl.ds(start, BAND, stride=dil), :] if dil > 1 else ref[pl.ds(start, BAND), :]
            q = rows(q_ref, cur).astype(jnp.bfloat16)
            kk = jnp.concatenate([rows(k_ref, prev), rows(k_ref, cur)], axis=0).astype(jnp.bfloat16)
            vv = jnp.concatenate([rows(v_ref, prev), rows(v_ref, cur)], axis=0).astype(jnp.bfloat16)
            s = _nt_dot(q, kk) + bias
            s = jnp.where((blk == 0) & (col < BAND), NEG, s)
            mx = jnp.max(s, axis=1, keepdims=True)
            e = jnp.exp(s - mx)
            den = jnp.sum(e, axis=1, keepdims=True)
            out = jnp.dot((e / den).astype(jnp.bfloat16), vv, preferred_element_type=jnp.float32)
            lse = jnp.broadcast_to(mx + jnp.log(den), (BAND, HEAD_DIM))
            if dil > 1:
                out_scr[p, pl.ds(cur, BAND, stride=dil), :] = out
                lse_scr[p, pl.ds(cur, BAND, stride=dil), :] = lse
            else:
                out_scr[p, pl.ds(cur, BAND), :] = out
                lse_scr[p, pl.ds(cur, BAND), :] = lse
            return 0

        lax.fori_loop(0, dil * n_blocks, band_block, 0, unroll=4)

    def combine(c, _):
        rs = pl.ds(pl.multiple_of(c * ATT_BLK, ATT_BLK), ATT_BLK)
        l0, l1, l2 = lse_scr[0, rs, :], lse_scr[1, rs, :], lse_scr[2, rs, :]
        mx = jnp.maximum(jnp.maximum(l0, l1), l2)
        w0, w1, w2 = jnp.exp(l0 - mx), jnp.exp(l1 - mx), jnp.exp(l2 - mx)
        tot = w0 + w1 + w2
        y = (w0 / tot) * out_scr[0, rs, :] + (w1 / tot) * out_scr[1, rs, :] + (w2 / tot) * out_scr[2, rs, :]
        o_ref[rs, :] = y.astype(o_ref.dtype)
        return 0

    lax.fori_loop(0, S // ATT_BLK, combine, 0)


def _dil_attention(qkv_c, bias_c, B, S):
    seq = lambda off: pl.BlockSpec((S, HEAD_DIM), lambda b, h: (b, off + h))
    return pl.pallas_call(
        _dil_body,
        grid=(B, N_HEADS),
        in_specs=[seq(0), seq(N_HEADS), seq(2 * N_HEADS),
                  pl.BlockSpec((len(DIL_PATTERNS), 1, BAND, 2 * BAND), lambda b, h: (0, h, 0, 0))],
        out_specs=seq(0),
        out_shape=jax.ShapeDtypeStruct((B * S, GROUP_W), jnp.bfloat16),
        scratch_shapes=[pltpu.VMEM((len(DIL_PATTERNS), S, HEAD_DIM), jnp.float32),
                        pltpu.VMEM((len(DIL_PATTERNS), S, HEAD_DIM), jnp.float32)],
        compiler_params=_cparams(("arbitrary", "arbitrary")),
        name="dilated_attention",
    )(qkv_c, qkv_c, qkv_c, bias_c)


def _rms(x, g):
    return x * lax.rsqrt(jnp.mean(jnp.square(x), axis=1, keepdims=True) + EPS) * g


def _mla_prep_body(u_ref, gq_ref, gkv_ref, wq_ref, wkv_ref, cos_ref, sin_ref, q_ref, k_ref, v_ref):
    cos = cos_ref[...]
    sin = sin_ref[...]
    half = MLA_ROPE // 2
    rope = lambda x: x * cos + _rot_partner(x, half) * sin
    cq = _rms(u_ref[:, :MLA_Q_RANK], gq_ref[...]).astype(jnp.bfloat16)
    ckv = _rms(u_ref[:, MLA_Q_RANK:MLA_Q_RANK + MLA_KV_RANK], gkv_ref[...]).astype(jnp.bfloat16)
    q = jnp.dot(cq, wq_ref[...], preferred_element_type=jnp.float32)
    kv = jnp.dot(ckv, wkv_ref[...], preferred_element_type=jnp.float32)
    k_rope = rope(u_ref[:, MLA_Q_RANK + MLA_KV_RANK:]).astype(k_ref.dtype)
    for h in range(N_HEADS):
        base = h * MLA_QK
        q_ref[:, base:base + MLA_NOPE] = q[:, base:base + MLA_NOPE].astype(q_ref.dtype)
        q_ref[:, base + MLA_NOPE:base + MLA_QK] = rope(q[:, base + MLA_NOPE:base + MLA_QK]).astype(q_ref.dtype)
        k_ref[:, base:base + MLA_NOPE] = kv[:, h * MLA_NOPE:(h + 1) * MLA_NOPE].astype(k_ref.dtype)
        k_ref[:, base + MLA_NOPE:base + MLA_QK] = k_rope
    v_ref[...] = kv[:, N_HEADS * MLA_NOPE:].astype(v_ref.dtype)


def _mla_prep(u_mla, g_q, g_kv, w_uq, w_ukv, cos_m, sin_m, tm=512):
    T = u_mla.shape[0]
    row = lambda w: pl.BlockSpec((tm, w), lambda i: (i, 0))
    full = lambda a: pl.BlockSpec(a.shape, lambda i: (0, 0))
    return pl.pallas_call(
        _mla_prep_body,
        grid=(T // tm,),
        in_specs=[row(MLA_W), full(g_q), full(g_kv), full(w_uq), full(w_ukv), row(LANES), row(LANES)],
        out_specs=[row(N_HEADS * MLA_QK), row(N_HEADS * MLA_QK), row(GROUP_W)],
        out_shape=[jax.ShapeDtypeStruct((T, N_HEADS * MLA_QK), jnp.bfloat16),
                   jax.ShapeDtypeStruct((T, N_HEADS * MLA_QK), jnp.bfloat16),
                   jax.ShapeDtypeStruct((T, GROUP_W), jnp.bfloat16)],
        compiler_params=_cparams(("arbitrary",)),
        name="mla_prep",
    )(u_mla, g_q, g_kv, w_uq, w_ukv, cos_m, sin_m)


def _mla_body(q_ref, k_ref, v_ref, o_ref):
    j = pl.program_id(1)
    qk_cols = [slice(h * MLA_QK, (h + 1) * MLA_QK) for h in range(N_HEADS)]
    v_cols = [slice(h * HEAD_DIM, (h + 1) * HEAD_DIM) for h in range(N_HEADS)]
    qs = [q_ref[:, c] for c in qk_cols]
    done = _sweep_key_tiles(j, N_HEADS,
                            lambda h, kb, ks: _nt_dot(qs[h], k_ref[pl.ds(ks, ATT_BLK), qk_cols[h]]),
                            lambda h, ks: v_ref[pl.ds(ks, ATT_BLK), v_cols[h]])
    for h, (_, l, acc) in enumerate(done):
        o_ref[:, v_cols[h]] = (acc / l).astype(o_ref.dtype)


def _mla_attention(q, k, v, B, S):
    nq = S // ATT_BLK
    return pl.pallas_call(
        _mla_body,
        grid=(B, nq),
        in_specs=[pl.BlockSpec((ATT_BLK, N_HEADS * MLA_QK), lambda b, j: (b * nq + j, 0)),
                  pl.BlockSpec((S, N_HEADS * MLA_QK), lambda b, j: (b, 0)),
                  pl.BlockSpec((S, GROUP_W), lambda b, j: (b, 0))],
        out_specs=pl.BlockSpec((ATT_BLK, GROUP_W), lambda b, j: (b * nq + j, 0)),
        out_shape=jax.ShapeDtypeStruct((B * S, GROUP_W), jnp.bfloat16),
        compiler_params=_cparams(("arbitrary", "arbitrary")),
        name="mla_attention",
    )(q, k, v)


def _layer_norm_rows(h, g, b):
    mu = jnp.mean(h, axis=1, keepdims=True)
    d = h - mu
    var = jnp.mean(jnp.square(d), axis=1, keepdims=True)
    return d * lax.rsqrt(var + EPS) * g + b


def _pack_bf16_pairs(x):
    n = x.shape[1] // 2
    lo = lax.bitcast_convert_type(x[:, :n].astype(jnp.bfloat16).astype(jnp.float32), jnp.int32)
    hi = lax.bitcast_convert_type(x[:, n:].astype(jnp.bfloat16).astype(jnp.float32), jnp.int32)
    return lax.shift_right_logical(lo, 16) | (hi & jnp.int32(-65536))


def _unpack_bf16_pairs(u):
    lo = lax.bitcast_convert_type(lax.shift_left(u, 16), jnp.float32).astype(jnp.bfloat16)
    hi = lax.bitcast_convert_type(u & jnp.int32(-65536), jnp.float32).astype(jnp.bfloat16)
    return lo, hi


def _wo_ln_body(ya_ref, yb_ref, yc_ref, yd_ref, wo_ref, x_ref, g_ref, b_ref, o_ref, op_ref, *, alpha):
    h = alpha * x_ref[...]
    for gi, y_ref in enumerate((ya_ref, yb_ref, yc_ref, yd_ref)):
        h = h + jnp.dot(y_ref[...], wo_ref[gi * GROUP_W:(gi + 1) * GROUP_W, :], preferred_element_type=jnp.float32)
    out = _layer_norm_rows(h, g_ref[...], b_ref[...])
    o_ref[...] = out
    op_ref[...] = _pack_bf16_pairs(out)


def _wo_ln(ys, w_o, x, g, b, alpha, tm=256):
    T = x.shape[0]
    row = lambda w: pl.BlockSpec((tm, w), lambda i: (i, 0))
    full = lambda a: pl.BlockSpec(a.shape, lambda i: (0, 0))
    return pl.pallas_call(
        functools.partial(_wo_ln_body, alpha=alpha),
        grid=(T // tm,),
        in_specs=[row(GROUP_W)] * 4 + [full(w_o), row(D_MODEL), full(g), full(b)],
        out_specs=[row(D_MODEL), row(D_MODEL // 2)],
        out_shape=[jax.ShapeDtypeStruct((T, D_MODEL), jnp.float32),
                   jax.ShapeDtypeStruct((T, D_MODEL // 2), jnp.int32)],
        compiler_params=_cparams(("arbitrary",)),
        name="wo_layernorm",
    )(*ys, w_o, x, g, b)


def _lane_slab(cols, dtype):
    rows = cols[0].shape[0]
    lane = lax.broadcasted_iota(jnp.int32, (rows, LANES), 1)
    out = jnp.zeros((rows, LANES), dtype)
    for i, c in enumerate(cols):
        out = jnp.where(lane == i, c.astype(dtype), out)
    return out


def _router_body(x_ref, w_ref, b_ref, idx_ref, gate_ref, rank_ref, cnt_ref, carry_scr):
    tm = x_ref.shape[0]

    @pl.when(pl.program_id(0) == 0)
    def _():
        carry_scr[...] = jnp.zeros_like(carry_scr)

    logits = jnp.dot(x_ref[...], w_ref[...], preferred_element_type=jnp.float32,
                     precision=lax.Precision.HIGHEST) + b_ref[...]
    lane = lax.broadcasted_iota(jnp.int32, logits.shape, 1)
    vals, idxs, hots = [], [], []
    for _ in range(TOP_K):
        m = jnp.max(logits, axis=1, keepdims=True)
        idx = jnp.min(jnp.where(logits == m, lane, N_EXPERTS), axis=1, keepdims=True)
        hot = lane == idx
        vals.append(m)
        idxs.append(idx)
        hots.append(hot)
        logits = jnp.where(hot, -jnp.inf, logits)
    exps = [jnp.exp(v - vals[0]) for v in vals]
    tot = exps[0] + exps[1] + exps[2] + exps[3]
    gates = [e / tot for e in exps]

    hot_f = [h.astype(jnp.float32) for h in hots]
    hot_all = hot_f[0] + hot_f[1] + hot_f[2] + hot_f[3]
    r = lax.broadcasted_iota(jnp.int32, (tm, tm), 0)
    c = lax.broadcasted_iota(jnp.int32, (tm, tm), 1)
    before = (c < r).astype(jnp.bfloat16)
    prior = jnp.dot(before, hot_all.astype(jnp.bfloat16), preferred_element_type=jnp.float32) + carry_scr[...]
    ranks = [jnp.sum(h * prior, axis=1, keepdims=True) for h in hot_f]
    carry_scr[...] = carry_scr[...] + jnp.sum(hot_all, axis=0, keepdims=True)

    idx_ref[...] = _lane_slab(idxs, jnp.int32)
    gate_ref[...] = _lane_slab(gates, jnp.float32)
    rank_ref[...] = _lane_slab(ranks, jnp.int32)
    cnt_ref[...] = carry_scr[...]


def _router(x, w_router, b_router, tm=512):
    T = x.shape[0]
    row = lambda w: pl.BlockSpec((tm, w), lambda i: (i, 0))
    full = lambda a: pl.BlockSpec(a.shape, lambda i: (0, 0))
    slab = lambda dt: jax.ShapeDtypeStruct((T, LANES), dt)
    return pl.pallas_call(
        _router_body,
        grid=(T // tm,),
        in_specs=[row(D_MODEL), full(w_router), full(b_router)],
        out_specs=[row(LANES), row(LANES), row(LANES), pl.BlockSpec((1, N_EXPERTS), lambda i: (0, 0))],
        out_shape=[slab(jnp.int32), slab(jnp.float32), slab(jnp.int32),
                   jax.ShapeDtypeStruct((1, N_EXPERTS), jnp.float32)],
        scratch_shapes=[pltpu.VMEM((1, N_EXPERTS), jnp.float32)],
        compiler_params=_cparams(("arbitrary",)),
        name="router",
    )(x, w_router, b_router)


def _dispatch_body(dest_ref, x_ref, zero_hbm, o_hbm, sem, *, tm):
    del zero_hbm

    def start(n, _):
        pltpu.make_async_copy(x_ref.at[pl.ds(n // TOP_K, 1)], o_hbm.at[pl.ds(dest_ref[n], 1)], sem).start()
        return 0

    lax.fori_loop(0, tm * TOP_K, start, 0, unroll=8)
    for _ in range(TOP_K):
        pltpu.make_async_copy(x_ref, o_hbm.at[pl.ds(0, tm)], sem).wait()


def _dispatch(dest_flat, x_packed, n_rows, tm=512):
    T, W = x_packed.shape
    zeros = jnp.zeros((n_rows, W), x_packed.dtype)
    return pl.pallas_call(
        functools.partial(_dispatch_body, tm=tm),
        grid=(T // tm,),
        in_specs=[pl.BlockSpec((tm * TOP_K,), lambda i: (i,), memory_space=pltpu.SMEM),
                  pl.BlockSpec((tm, W), lambda i: (i, 0)), pl.BlockSpec(memory_space=pl.ANY)],
        out_specs=pl.BlockSpec(memory_space=pl.ANY),
        out_shape=jax.ShapeDtypeStruct((n_rows, W), x_packed.dtype),
        scratch_shapes=[pltpu.SemaphoreType.DMA(())],
        input_output_aliases={2: 0},
        compiler_params=_cparams(("arbitrary",)),
        name="moe_dispatch",
    )(dest_flat, x_packed, zeros)


def _ffn_body(be_ref, nu_ref, x_ref, wgu_ref, bgu_ref, wdn_ref, bdn_ref, y_ref):
    del be_ref

    @pl.when(pl.program_id(0) < nu_ref[0])
    def _():
        lo, hi = _unpack_bf16_pairs(x_ref[...])
        half = D_MODEL // 2
        h = (jnp.dot(lo, wgu_ref[0, :half, :], preferred_element_type=jnp.float32)
             + jnp.dot(hi, wgu_ref[0, half:, :], preferred_element_type=jnp.float32) + bgu_ref[0])
        glu = jnp.minimum(h[:, :D_EXPERT], SWIGLU_LIMIT)
        lin = jnp.clip(h[:, D_EXPERT:], -SWIGLU_LIMIT, SWIGLU_LIMIT)
        a = glu * jax.nn.sigmoid(SWIGLU_ALPHA * glu) * (lin + 1.0)
        y_ref[...] = jnp.dot(a.astype(jnp.bfloat16), wdn_ref[0], preferred_element_type=jnp.float32) + bdn_ref[0]

    @pl.when(pl.program_id(0) >= nu_ref[0])
    def _():
        y_ref[...] = jnp.zeros_like(y_ref)


def _expert_ffn(block_e, n_used, xg, w_gu, b_gu, w_dn, b_dn):
    P = xg.shape[0]
    n_blocks = P // MOE_BLK
    per_e = lambda a: pl.BlockSpec((1,) + a.shape[1:], lambda i, be, nu: (be[i], 0, 0))
    grid_spec = pltpu.PrefetchScalarGridSpec(
        num_scalar_prefetch=2,
        grid=(n_blocks,),
        in_specs=[pl.BlockSpec((MOE_BLK, D_MODEL // 2), lambda i, be, nu: (i, 0)),
                  per_e(w_gu), per_e(b_gu), per_e(w_dn), per_e(b_dn)],
        out_specs=pl.BlockSpec((MOE_BLK, D_MODEL), lambda i, be, nu: (i, 0)),
    )
    return pl.pallas_call(
        _ffn_body,
        grid_spec=grid_spec,
        out_shape=jax.ShapeDtypeStruct((P, D_MODEL), jnp.float32),
        compiler_params=_cparams(("arbitrary",)),
        name="expert_ffn",
    )(block_e, n_used, xg, w_gu, b_gu, w_dn, b_dn)


def _combine_body(dest_ref, dest_next_ref, y_hbm, x_ref, gate_ref, g_ref, b_ref, o_ref, ob_ref, buf, sem, *, alpha, tm):
    i = pl.program_id(0)
    slot = i % 2

    def gather_rows(idx_ref, s):
        def start(n, _):
            pltpu.make_async_copy(y_hbm.at[pl.ds(idx_ref[n], 1)], buf.at[s, n % TOP_K, pl.ds(n // TOP_K, 1)],
                                  sem.at[s]).start()
            return 0
        lax.fori_loop(0, tm * TOP_K, start, 0, unroll=8)

    @pl.when(i == 0)
    def _():
        gather_rows(dest_ref, 0)

    @pl.when(i + 1 < pl.num_programs(0))
    def _():
        gather_rows(dest_next_ref, 1 - slot)

    for jj in range(TOP_K):
        pltpu.make_async_copy(y_hbm.at[pl.ds(0, tm)], buf.at[slot, jj], sem.at[slot]).wait()

    h = alpha * x_ref[...]
    gates = gate_ref[...]
    for jj in range(TOP_K):
        h = h + gates[:, jj:jj + 1] * buf[slot, jj]
    out = _layer_norm_rows(h, g_ref[...], b_ref[...])
    o_ref[...] = out
    ob_ref[...] = out.astype(ob_ref.dtype)


def _combine(dest_flat, y, x, gates, g, b, alpha, tm=128):
    T = x.shape[0]
    n_tiles = T // tm
    row = lambda w: pl.BlockSpec((tm, w), lambda i: (i, 0))
    full = lambda a: pl.BlockSpec(a.shape, lambda i: (0, 0))
    return pl.pallas_call(
        functools.partial(_combine_body, alpha=alpha, tm=tm),
        grid=(n_tiles,),
        in_specs=[pl.BlockSpec((tm * TOP_K,), lambda i: (i,), memory_space=pltpu.SMEM),
                  pl.BlockSpec((tm * TOP_K,), lambda i: (jnp.minimum(i + 1, n_tiles - 1),), memory_space=pltpu.SMEM),
                  pl.BlockSpec(memory_space=pl.ANY), row(D_MODEL), row(LANES), full(g), full(b)],
        out_specs=[row(D_MODEL), row(D_MODEL)],
        out_shape=[jax.ShapeDtypeStruct((T, D_MODEL), jnp.float32),
                   jax.ShapeDtypeStruct((T, D_MODEL), jnp.bfloat16)],
        scratch_shapes=[pltpu.VMEM((2, TOP_K, tm, D_MODEL), jnp.float32), pltpu.SemaphoreType.DMA((2,))],
        compiler_params=_cparams(("arbitrary",)),
        name="moe_combine",
    )(dest_flat, dest_flat, y, x, gates, g, b)


def _t5_bucket(dist):
    n = jnp.maximum(dist, 0)
    exact = T5_BUCKETS // 2
    nf = jnp.maximum(n, 1).astype(jnp.float32)
    large = exact + (jnp.log(nf / exact) / math.log(T5_MAX_DIST / exact) * (T5_BUCKETS - exact)).astype(jnp.int32)
    return jnp.where(n < exact, n, jnp.minimum(large, T5_BUCKETS - 1))


def _toeplitz_bias(rel, S):
    n_heads = rel.shape[1]
    period = S + ATT_BLK
    k = jnp.arange(period)
    dist = (S - ATT_BLK) - jnp.where(k < S, k, k - period)
    by_dist = jnp.transpose(rel[_t5_bucket(jnp.arange(S))].astype(jnp.float32))
    row0 = jnp.where((dist >= 0) & (dist < S), by_dist[:, jnp.clip(dist, 0, S - 1)], 0.0)
    flat = jnp.tile(row0, (1, ATT_BLK + 1))[:, :ATT_BLK * (period - 1)]
    return flat.reshape(n_heads, ATT_BLK, period - 1)[:, :, :S]


def _band_bias(rel):
    m = BAND + jnp.arange(BAND)[:, None] - jnp.arange(2 * BAND)[None, :]
    out = []
    for window, dil in DIL_PATTERNS:
        keep = (m >= 0) & (m <= window // dil)
        tab = jnp.transpose(rel[_t5_bucket(m * dil)].astype(jnp.float32), (2, 0, 1))
        out.append(jnp.where(keep[None], tab, NEG))
    return jnp.stack(out, 0)


def _layer_weights(w_in_l, w_uq_l, w_ukv_l):
    bf = jnp.bfloat16
    o = np.cumsum((0, 512, 512, 512, 1024, 64, 16, 512, 512, 512, 512, 512, 512, 384, 256, 64))
    seg = lambda a, b: w_in_l[:, int(o[a]):int(o[b])]
    sa, sb, sd = HEAD_DIM ** -0.5, DIFF_QK ** -0.5, (MLA_NOPE + MLA_ROPE) ** -0.5
    w_ab = jnp.concatenate([seg(0, 1) * sa, seg(1, 3), seg(6, 7) * sb, seg(7, 9)], axis=1).astype(bf)
    w_c = jnp.concatenate([seg(9, 10) * sa, seg(10, 12)], axis=1).astype(bf)
    w_uq_l = w_uq_l * sd
    pad = lambda n: jnp.zeros((D_MODEL, n), w_in_l.dtype)
    w_idx = jnp.concatenate([seg(3, 6), pad(IDX_W - 1104)], axis=1).astype(bf)
    w_mla = jnp.concatenate([seg(12, 15), pad(MLA_W - 704)], axis=1).astype(bf)
    uq = w_uq_l.reshape(MLA_Q_RANK, N_HEADS, MLA_NOPE + MLA_ROPE)
    uq = jnp.pad(uq, ((0, 0), (0, 0), (0, MLA_QK - MLA_NOPE - MLA_ROPE))).reshape(MLA_Q_RANK, N_HEADS * MLA_QK)
    ukv = w_ukv_l.reshape(MLA_KV_RANK, N_HEADS, 2, MLA_NOPE)
    ukv = jnp.transpose(ukv, (0, 2, 1, 3)).reshape(MLA_KV_RANK, 2 * N_HEADS * MLA_NOPE)
    return w_ab, w_c, w_idx, w_mla, uq.astype(bf), ukv.astype(bf)


def kernel(x, positions, rel_bias, w_in, w_o, ln1_g, ln1_b, ln2_g, ln2_b, mla_g_q, mla_w_uq, mla_g_kv, mla_w_ukv,
           diff_lq1, diff_lk1, diff_lq2, diff_lk2, diff_subln_g, w_router, b_router, w_gate_up, b_gate_up, w_down,
           b_down):
    B, S, D = x.shape
    depth = w_in.shape[0]
    T = B * S
    assert D == D_MODEL and S % (16 * BAND) == 0 and T % 1024 == 0
    alpha = (2 * depth) ** 0.25
    f32 = jnp.float32

    bias_a = _toeplitz_bias(rel_bias[:, :N_HEADS], S)
    bias_b = _toeplitz_bias(rel_bias[:, N_HEADS:2 * N_HEADS], S)
    bias_c = _band_bias(rel_bias[:, 2 * N_HEADS:])
    cos_i, sin_i = _rope_tables(positions, IDX_ROPE // 2, IDX_DIM)
    cos_m, sin_m = _rope_tables(positions, MLA_ROPE // 2, LANES)

    n_blocks = T * TOP_K // MOE_BLK + N_EXPERTS
    n_rows = n_blocks * MOE_BLK

    xf = x.reshape(T, D)
    xb = xf.astype(jnp.bfloat16)
    for l in range(depth):
        w_ab, w_c, w_idx, w_mla, w_uq, w_ukv = _layer_weights(w_in[l], mla_w_uq[l], mla_w_ukv[l])
        u_ab = _matmul(xb, w_ab, jnp.bfloat16, 1024, 512)
        u_c = _matmul(xb, w_c, f32, 1024, 512)
        u_idx = _matmul(xb, w_idx, f32, 1024, IDX_W // 3)
        u_mla = _matmul(xb, w_mla, f32, 1024, MLA_W)

        qi, kie, kio, w_i = _idx_prep(u_idx, cos_i, sin_i)
        ya = _dsa_attention(qi, kie, kio, w_i, u_ab, bias_a, B, S)

        lam_init = 0.8 - 0.6 * math.exp(-0.3 * l)
        lam = (jnp.exp(jnp.sum(diff_lq1[l].astype(f32) * diff_lk1[l].astype(f32)))
               - jnp.exp(jnp.sum(diff_lq2[l].astype(f32) * diff_lk2[l].astype(f32))) + lam_init).reshape(1)
        yb = _diff_attention(lam, u_ab, bias_b, diff_subln_g[l].reshape(1, HEAD_DIM).astype(f32), lam_init, B, S)

        yc = _dil_attention(u_c, bias_c, B, S)

        qd, kd, vd = _mla_prep(u_mla, mla_g_q[l].reshape(1, -1).astype(f32), mla_g_kv[l].reshape(1, -1).astype(f32),
                               w_uq, w_ukv, cos_m, sin_m)
        yd = _mla_attention(qd, kd, vd, B, S)

        x1, x1p = _wo_ln((ya, yb, yc, yd), w_o[l].astype(jnp.bfloat16), xf,
                         ln1_g[l].reshape(1, D).astype(f32), ln1_b[l].reshape(1, D).astype(f32), alpha)

        idx_s, gate_s, rank_s, counts = _router(x1, w_router[l].astype(f32), b_router[l].reshape(1, -1).astype(f32))
        counts = counts.reshape(N_EXPERTS).astype(jnp.int32)
        padded = (counts + MOE_BLK - 1) // MOE_BLK * MOE_BLK
        pad_end = jnp.cumsum(padded)
        pad_start = pad_end - padded
        dest = (pad_start[idx_s[:, :TOP_K]] + rank_s[:, :TOP_K]).reshape(T * TOP_K).astype(jnp.int32)
        block_e = jnp.minimum(jnp.searchsorted(pad_end, jnp.arange(n_blocks) * MOE_BLK, side='right'),
                              N_EXPERTS - 1).astype(jnp.int32)
        n_used = (pad_end[-1:] // MOE_BLK).astype(jnp.int32)

        xg = _dispatch(dest, x1p, n_rows)
        y = _expert_ffn(block_e, n_used, xg, w_gate_up[l].astype(jnp.bfloat16),
                        b_gate_up[l].reshape(N_EXPERTS, 1, -1).astype(f32), w_down[l].astype(jnp.bfloat16),
                        b_down[l].reshape(N_EXPERTS, 1, -1).astype(f32))
        xf, xb = _combine(dest, y, x1, gate_s, ln2_g[l].reshape(1, D).astype(f32),
                          ln2_b[l].reshape(1, D).astype(f32), alpha)
    return xf.reshape(B, S, D)
```
